```python
import math
import jax
import jax.numpy as jnp
from jax import lax
import numpy as np

D_MODEL = 1024
BATCH = 16
SEQ = 4096
DEPTH = 2

HEAD_DIM = 64
SB_HEADS = D_MODEL // (2 * HEAD_DIM)
MOBA_HEADS = D_MODEL // (2 * HEAD_DIM)
SB_WIDTH = SB_HEADS * HEAD_DIM
MOBA_WIDTH = MOBA_HEADS * HEAD_DIM
ATT_IN_WIDTH = 3 * SB_WIDTH + 3 * MOBA_WIDTH + SB_WIDTH + MOBA_WIDTH
ATT_OUT_WIDTH = SB_WIDTH + MOBA_WIDTH
SB_BLOCK = 128
MOBA_BLOCK = 256
MOBA_TOPK = 3
MOBA_Q_CHUNK = 16
ROPE_THETA = 500000.0
ROT_DIM = HEAD_DIM // 4
LRU_WIDTH = D_MODEL
LRU_BLOCKS = 8
LRU_BLOCK_WIDTH = LRU_WIDTH // LRU_BLOCKS
CONV_WIDTH = 4
LRU_C = 8.0
N_ATT_LAYERS = (DEPTH + 1) // 2
N_LRU_LAYERS = DEPTH // 2
EPS = 1e-6

kernel_name = "hybrid_sb_moba_rglru_adaln"


def rmsnorm(x, g):
    x32 = x.astype(jnp.float32)
    xn = x32 * lax.rsqrt(jnp.mean(x32 * x32, axis=-1, keepdims=True) + EPS)
    return xn * g.astype(jnp.float32)


def rope_partial(t, positions):
    half = ROT_DIM // 2
    inv = ROPE_THETA ** (-jnp.arange(0, ROT_DIM, 2, dtype=jnp.float32) / ROT_DIM)
    ang = positions.astype(jnp.float32)[:, None, :, None] * inv
    cos, sin = jnp.cos(ang), jnp.sin(ang)
    tr = t[..., :ROT_DIM].astype(jnp.float32)
    t1, t2 = tr[..., :half], tr[..., half:]
    rot = jnp.concatenate([t1 * cos - t2 * sin, t2 * cos + t1 * sin], axis=-1)
    return jnp.concatenate([rot.astype(t.dtype), t[..., ROT_DIM:]], axis=-1)


def stick_breaking_attention(q, k, v):
    S = q.shape[2]
    scale = HEAD_DIM ** -0.5
    outs = []
    for blk in range(S // SB_BLOCK):
        t0, t1 = blk * SB_BLOCK, (blk + 1) * SB_BLOCK
        qb = q[:, :, t0:t1].astype(jnp.float32)
        kb = k[:, :, :t1].astype(jnp.float32)
        vb = v[:, :, :t1].astype(jnp.float32)
        z = jnp.einsum('bhqd,bhkd->bhqk', qb, kb) * scale
        mask = jnp.arange(t1)[None, :] < jnp.arange(t0, t1)[:, None]
        log_1m = jnp.where(mask, jax.nn.log_sigmoid(-z), 0.0)
        later = lax.cumsum(log_1m, axis=3, reverse=True) - log_1m
        w = jnp.where(mask, jnp.exp(jax.nn.log_sigmoid(z) + later), 0.0)
        outs.append(jnp.einsum('bhqk,bhkd->bhqd', w, vb))
    return jnp.concatenate(outs, axis=2).astype(q.dtype)


def moba_attention(q, k, v):
    B, H, S, dh = q.shape
    nb = -(-S // MOBA_BLOCK)
    pad = nb * MOBA_BLOCK - S
    kp = jnp.pad(k, ((0, 0), (0, 0), (0, pad), (0, 0)))
    vp = jnp.pad(v, ((0, 0), (0, 0), (0, pad), (0, 0)))
    kb = kp.reshape(B, H, nb, MOBA_BLOCK, dh)
    vb = vp.reshape(B, H, nb, MOBA_BLOCK, dh)
    kmean = jnp.mean(kb.astype(jnp.float32), axis=3)
    gscore = jnp.einsum('bhsd,bhnd->bhsn', q.astype(jnp.float32), kmean)
    qblk = jnp.arange(S) // MOBA_BLOCK
    past = jnp.arange(nb)[None, :] < qblk[:, None]
    gscore = jnp.where(past, gscore, -jnp.inf)
    n_sel = min(MOBA_TOPK, nb)
    _, sel = lax.top_k(gscore, n_sel)
    sel_valid = sel < qblk[:, None]
    bi = jnp.arange(B)[:, None, None, None]
    hi = jnp.arange(H)[None, :, None, None]
    scale = dh ** -0.5

    def chunk(ci):
        t0 = ci * MOBA_Q_CHUNK
        qc = lax.dynamic_slice_in_dim(q, t0, MOBA_Q_CHUNK, axis=2).astype(jnp.float32)
        sel_c = lax.dynamic_slice_in_dim(sel, t0, MOBA_Q_CHUNK, axis=2)
        val_c = lax.dynamic_slice_in_dim(sel_valid, t0, MOBA_Q_CHUNK, axis=2)
        k_sel = kb[bi, hi, sel_c].astype(jnp.float32)
        v_sel = vb[bi, hi, sel_c].astype(jnp.float32)
        s_sel = jnp.einsum('bhqd,bhqnkd->bhqnk', qc, k_sel) * scale
        s_sel = jnp.where(val_c[..., None], s_sel, -jnp.inf)
        s_sel = s_sel.reshape(B, H, MOBA_Q_CHUNK, n_sel * MOBA_BLOCK)
        ob = t0 // MOBA_BLOCK
        k_own = lax.dynamic_index_in_dim(kb, ob, axis=2, keepdims=False).astype(jnp.float32)
        v_own = lax.dynamic_index_in_dim(vb, ob, axis=2, keepdims=False).astype(jnp.float32)
        s_own = jnp.einsum('bhqd,bhkd->bhqk', qc, k_own) * scale
        qpos = t0 + jnp.arange(MOBA_Q_CHUNK)
        kpos = ob * MOBA_BLOCK + jnp.arange(MOBA_BLOCK)
        s_own = jnp.where(kpos[None, :] <= qpos[:, None], s_own, -jnp.inf)
        p = jax.nn.softmax(jnp.concatenate([s_sel, s_own], axis=-1), axis=-1)
        p_sel = p[..., :n_sel * MOBA_BLOCK].reshape(B, H, MOBA_Q_CHUNK, n_sel, MOBA_BLOCK)
        p_own = p[..., n_sel * MOBA_BLOCK:]
        return (jnp.einsum('bhqnk,bhqnkd->bhqd', p_sel, v_sel)
                + jnp.einsum('bhqk,bhkd->bhqd', p_own, v_own))

    out = lax.map(chunk, jnp.arange(S // MOBA_Q_CHUNK))
    out = out.transpose(1, 2, 0, 3, 4).reshape(B, H, S, dh)
    return out.astype(q.dtype)


def attention_layer(h, positions, w_in, w_out):
    B, S, _ = h.shape
    u = h @ w_in
    cuts = np.cumsum([SB_WIDTH] * 3 + [MOBA_WIDTH] * 3 + [SB_WIDTH])
    q_a, k_a, v_a, q_b, k_b, v_b, g_a, g_b = jnp.split(u, list(cuts), axis=-1)

    def heads(t, n):
        return t.reshape(B, S, n, HEAD_DIM).transpose(0, 2, 1, 3)

    def merge(t):
        return t.transpose(0, 2, 1, 3).reshape(B, S, -1)

    o_a = stick_breaking_attention(heads(q_a, SB_HEADS), heads(k_a, SB_HEADS), heads(v_a, SB_HEADS))
    qr = rope_partial(heads(q_b, MOBA_HEADS), positions)
    kr = rope_partial(heads(k_b, MOBA_HEADS), positions)
    o_b = moba_attention(qr, kr, heads(v_b, MOBA_HEADS))
    y = jnp.concatenate([merge(o_a) * jax.nn.silu(g_a), merge(o_b) * jax.nn.silu(g_b)], axis=-1)
    return y @ w_out


def rglru_layer(h, w_in, conv_w, conv_b, w_a, b_a, w_x, b_x, lam, w_out):
    B, S, _ = h.shape
    u = h @ w_in
    xb, g = u[..., :LRU_WIDTH], u[..., LRU_WIDTH:]
    xc = lax.conv_general_dilated(
        xb, conv_w[:, None, :].astype(xb.dtype), window_strides=(1,),
        padding=[(CONV_WIDTH - 1, 0)], dimension_numbers=('NWC', 'WIO', 'NWC'),
        feature_group_count=LRU_WIDTH) + conv_b
    xg = xc.reshape(B, S, LRU_BLOCKS, LRU_BLOCK_WIDTH)
    r = jax.nn.sigmoid(jnp.einsum('bsnc,ncd->bsnd', xg, w_a).reshape(B, S, LRU_WIDTH) + b_a)
    i = jax.nn.sigmoid(jnp.einsum('bsnc,ncd->bsnd', xg, w_x).reshape(B, S, LRU_WIDTH) + b_x)
    log_a = LRU_C * r.astype(jnp.float32) * jax.nn.log_sigmoid(lam.astype(jnp.float32))
    a = jnp.exp(log_a)
    mult = jnp.sqrt(-jnp.expm1(2.0 * log_a))
    bterm = mult * (i * xc).astype(jnp.float32)

    def combine(e1, e2):
        a1, b1 = e1
        a2, b2 = e2
        return a1 * a2, a2 * b1 + b2

    _, hs = lax.associative_scan(combine, (a, bterm), axis=1)
    y = hs.astype(h.dtype) * jax.nn.silu(g)
    return y @ w_out


def setup_inputs(seed: int = 0) -> dict:
    key = jax.random.key(seed)
    ks = jax.random.split(key, 20)
    f32 = jnp.float32
    D = D_MODEL
    x = jax.random.normal(ks[0], (BATCH, SEQ, D), f32)
    c = jax.random.normal(ks[1], (BATCH, D), f32)
    positions = jnp.broadcast_to(jnp.arange(SEQ, dtype=jnp.int32)[None, :], (BATCH, SEQ))
    norm_g = 1.0 + 0.02 * jax.random.normal(ks[2], (DEPTH, D), f32)
    w_mod = 0.5 * D ** -0.5 * jax.random.normal(ks[3], (DEPTH, D, 3 * D), f32)
    b_mod = 0.02 * jax.random.normal(ks[4], (DEPTH, 3 * D), f32)
    attn_w_in = D ** -0.5 * jax.random.normal(ks[5], (N_ATT_LAYERS, D, ATT_IN_WIDTH), f32)
    attn_w_out = ATT_OUT_WIDTH ** -0.5 * jax.random.normal(ks[6], (N_ATT_LAYERS, ATT_OUT_WIDTH, D), f32)
    lru_w_in = D ** -0.5 * jax.random.normal(ks[7], (N_LRU_LAYERS, D, 2 * LRU_WIDTH), f32)
    lru_conv_w = CONV_WIDTH ** -0.5 * jax.random.normal(ks[8], (N_LRU_LAYERS, CONV_WIDTH, LRU_WIDTH), f32)
    lru_conv_b = 0.02 * jax.random.normal(ks[9], (N_LRU_LAYERS, LRU_WIDTH), f32)
    lru_w_a = LRU_BLOCK_WIDTH ** -0.5 * jax.random.normal(
        ks[10], (N_LRU_LAYERS, LRU_BLOCKS, LRU_BLOCK_WIDTH, LRU_BLOCK_WIDTH), f32)
    lru_b_a = 0.02 * jax.random.normal(ks[11], (N_LRU_LAYERS, LRU_WIDTH), f32)
    lru_w_x = LRU_BLOCK_WIDTH ** -0.5 * jax.random.normal(
        ks[12], (N_LRU_LAYERS, LRU_BLOCKS, LRU_BLOCK_WIDTH, LRU_BLOCK_WIDTH), f32)
    lru_b_x = 0.02 * jax.random.normal(ks[13], (N_LRU_LAYERS, LRU_WIDTH), f32)
    a0 = jax.random.uniform(ks[14], (N_LRU_LAYERS, LRU_WIDTH), f32, minval=0.9, maxval=0.999)
    p = a0 ** (1.0 / LRU_C)
    lru_lambda = jnp.log(p) - jnp.log1p(-p)
    lru_w_out = LRU_WIDTH ** -0.5 * jax.random.normal(ks[15], (N_LRU_LAYERS, LRU_WIDTH, D), f32)
    final_g = 1.0 + 0.02 * jax.random.normal(ks[16], (D,), f32)
    return {"x": x, "c": c, "positions": positions, "norm_g": norm_g,
            "w_mod": w_mod, "b_mod": b_mod,
            "attn_w_in": attn_w_in, "attn_w_out": attn_w_out,
            "lru_w_in": lru_w_in, "lru_conv_w": lru_conv_w, "lru_conv_b": lru_conv_b,
            "lru_w_a": lru_w_a, "lru_b_a": lru_b_a, "lru_w_x": lru_w_x, "lru_b_x": lru_b_x,
            "lru_lambda": lru_lambda, "lru_w_out": lru_w_out, "final_g": final_g}


def reference(x, c, positions, norm_g, w_mod, b_mod, attn_w_in, attn_w_out,
              lru_w_in, lru_conv_w, lru_conv_b, lru_w_a, lru_b_a, lru_w_x, lru_b_x,
              lru_lambda, lru_w_out, final_g):
    for l in range(DEPTH):
        mod = c @ w_mod[l] + b_mod[l]
        shift, scale, gate = jnp.split(mod[:, None, :], 3, axis=-1)
        h = (rmsnorm(x, norm_g[l]) * (1.0 + scale.astype(jnp.float32))
             + shift.astype(jnp.float32)).astype(x.dtype)
        j = l // 2
        if l % 2 == 0:
            y = attention_layer(h, positions, attn_w_in[j], attn_w_out[j])
        else:
            y = rglru_layer(h, lru_w_in[j], lru_conv_w[j], lru_conv_b[j], lru_w_a[j],
                            lru_b_a[j], lru_w_x[j], lru_b_x[j], lru_lambda[j], lru_w_out[j])
        x = x + gate * y
    return rmsnorm(x, final_g).astype(x.dtype)
```

```python
import functools

import jax
import jax.numpy as jnp
import numpy as np
from jax import lax
from jax.experimental import pallas as pl
from jax.experimental.pallas import tpu as pltpu

F32 = jnp.float32
BF16 = jnp.bfloat16

D_MODEL = 1024
DEPTH = 2
HEAD_DIM = 64
N_HEADS = 8
GROUP_WIDTH = N_HEADS * HEAD_DIM
LANES = 128
HEAD_PAIRS = GROUP_WIDTH // LANES
SB_BLOCK = 128
MOBA_BLOCK = 256
MOBA_TOPK = 3
ROPE_THETA = 500000.0
ROT_DIM = HEAD_DIM // 4
LRU_BLOCKS = 8
LRU_BLOCK_WIDTH = D_MODEL // LRU_BLOCKS
CONV_WIDTH = 4
LRU_C = 8.0
EPS = 1e-6

PROJ_ROWS = 512
LRU_BATCH = 8
LRU_STEPS = 64
LRU_PITCH = LRU_STEPS + 8
VMEM_LIMIT = 56 * 1024 * 1024

_NT = (((1,), (1,)), ((), ()))


def _resident(shape, index_map):
    return pl.BlockSpec(shape, index_map, pipeline_mode=pl.Buffered(1))


def _rms(x, g):
    ms = jnp.mean(x * x, axis=-1, keepdims=True)
    return x * lax.rsqrt(ms + EPS) * g


def _mod_kernel(c_ref, w_ref, b_ref, o_ref):
    o_ref[...] = jnp.dot(c_ref[...], w_ref[...], preferred_element_type=F32,
                         precision=lax.Precision.HIGHEST) + b_ref[...]


def _modulation(c, w_mod, b_mod):
    batch = c.shape[0]
    return pl.pallas_call(
        _mod_kernel,
        grid=(DEPTH, 3),
        in_specs=[
            pl.BlockSpec((batch, D_MODEL), lambda l, j: (0, 0)),
            pl.BlockSpec((None, D_MODEL, D_MODEL), lambda l, j: (l, 0, j)),
            pl.BlockSpec((None, 1, D_MODEL), lambda l, j: (l, 0, j)),
        ],
        out_specs=pl.BlockSpec((None, batch, D_MODEL), lambda l, j: (l, 0, j)),
        out_shape=jax.ShapeDtypeStruct((DEPTH, batch, 3 * D_MODEL), F32),
        name="adaln_modulation",
    )(c, w_mod, b_mod.reshape(DEPTH, 1, 3 * D_MODEL))


def _proj_kernel(x_ref, mod_ref, g_ref, pos_ref, inv_ref, w_ref,
                 qa_ref, ka_ref, va_ref, qb_ref, kb_ref, vb_ref, ga_ref, gb_ref):
    shift = mod_ref[:, 0:D_MODEL]
    scale = mod_ref[:, D_MODEL:2 * D_MODEL]
    h = (_rms(x_ref[...], g_ref[...]) * (1.0 + scale) + shift).astype(BF16)

    def proj(j):
        return jnp.dot(h, w_ref[:, j * GROUP_WIDTH:(j + 1) * GROUP_WIDTH],
                       preferred_element_type=F32)

    qk_scale = HEAD_DIM ** -0.5
    qa_ref[...] = (proj(0) * qk_scale).astype(BF16)
    ka_ref[...] = proj(1).astype(BF16)
    va_ref[...] = proj(2).astype(BF16)
    vb_ref[...] = proj(5).astype(BF16)
    ga_ref[...] = jax.nn.silu(proj(6)).astype(BF16)
    gb_ref[...] = jax.nn.silu(proj(7)).astype(BF16)
    ang = pos_ref[...].astype(F32) * inv_ref[...]
    cos, sin = jnp.cos(ang), jnp.sin(ang)
    qb_ref[...] = ((proj(3) * cos + proj(8) * sin) * qk_scale).astype(BF16)
    kb_ref[...] = (proj(4) * cos + proj(9) * sin).astype(BF16)


def _rope_partner(w):
    w = w.reshape(D_MODEL, N_HEADS, HEAD_DIM)
    half = ROT_DIM // 2
    part = jnp.zeros_like(w)
    part = part.at[:, :, :half].set(-w[:, :, half:ROT_DIM])
    part = part.at[:, :, half:ROT_DIM].set(w[:, :, :half])
    return part.reshape(D_MODEL, GROUP_WIDTH)


def _rope_inv_lanes():
    half = ROT_DIM // 2
    inv = ROPE_THETA ** (-np.arange(0, ROT_DIM, 2, dtype=np.float32) / ROT_DIM)
    per_head = np.zeros((HEAD_DIM,), np.float32)
    per_head[:half] = inv
    per_head[half:ROT_DIM] = inv
    return jnp.asarray(np.tile(per_head, N_HEADS).reshape(1, GROUP_WIDTH))


def _attn_projection(x, mod, norm_g, positions, w_in):
    batch, seq, _ = x.shape
    w_q_b = w_in[:, 3 * GROUP_WIDTH:4 * GROUP_WIDTH]
    w_k_b = w_in[:, 4 * GROUP_WIDTH:5 * GROUP_WIDTH]
    w_all = jnp.concatenate([w_in, _rope_partner(w_q_b), _rope_partner(w_k_b)], axis=1).astype(BF16)
    n_cols = w_all.shape[1]
    out_spec = pl.BlockSpec((None, PROJ_ROWS, GROUP_WIDTH), lambda b, i: (b, i, 0))
    out_shape = jax.ShapeDtypeStruct((batch, seq, GROUP_WIDTH), BF16)
    return pl.pallas_call(
        _proj_kernel,
        grid=(batch, seq // PROJ_ROWS),
        in_specs=[
            pl.BlockSpec((None, PROJ_ROWS, D_MODEL), lambda b, i: (b, i, 0)),
            pl.BlockSpec((None, None, 1, 3 * D_MODEL), lambda b, i: (0, b, 0, 0)),
            pl.BlockSpec((None, 1, D_MODEL), lambda b, i: (0, 0, 0)),
            pl.BlockSpec((None, PROJ_ROWS, 1), lambda b, i: (b, i, 0)),
            pl.BlockSpec((1, GROUP_WIDTH), lambda b, i: (0, 0)),
            _resident((D_MODEL, n_cols), lambda b, i: (0, 0)),
        ],
        out_specs=[out_spec] * 8,
        out_shape=[out_shape] * 8,
        compiler_params=pltpu.CompilerParams(
            dimension_semantics=("parallel", "parallel"), vmem_limit_bytes=VMEM_LIMIT),
        name="attn_in_projection",
    )(x, mod.reshape(DEPTH, batch, 1, 3 * D_MODEL), norm_g.reshape(DEPTH, 1, D_MODEL),
      positions.reshape(batch, seq, 1), _rope_inv_lanes(), w_all)


def _head_lane_masks():
    lane = lax.broadcasted_iota(jnp.int32, (1, LANES), 1)
    return lane < HEAD_DIM, lane >= HEAD_DIM


def _sb_kernel(q_ref, k_ref, v_ref, o_ref):
    qi = pl.program_id(2)
    blk = SB_BLOCK
    q = q_ref[...]
    row = lax.broadcasted_iota(jnp.int32, (blk, blk), 0)
    col = lax.broadcasted_iota(jnp.int32, (blk, blk), 1)
    below_diag = col < row
    later_and_total = jnp.concatenate(
        [(row > col).astype(BF16), jnp.ones((blk, blk), BF16)], axis=1)

    def head(lane_mask):
        qh = jnp.where(lane_mask, q, jnp.zeros_like(q))

        def block(kb, carry, diagonal):
            acc, run = carry
            start = pl.multiple_of(kb * blk, blk)
            k = k_ref[pl.ds(start, blk), :]
            v = v_ref[pl.ds(start, blk), :]
            z = lax.dot_general(qh, k, _NT, preferred_element_type=F32)
            lp = jnp.log(1.0 + jnp.exp(-jnp.abs(z)))
            log_1m = jnp.minimum(-z, 0.0) - lp
            log_b = jnp.minimum(z, 0.0) - lp
            if diagonal:
                log_1m = jnp.where(below_diag, log_1m, 0.0)
            hi = log_1m.astype(BF16)
            lo = (log_1m - hi.astype(F32)).astype(BF16)
            sums = (jnp.dot(hi, later_and_total, preferred_element_type=F32)
                    + jnp.dot(lo, later_and_total, preferred_element_type=F32))
            w = jnp.exp(log_b + sums[:, :blk] + run)
            if diagonal:
                w = jnp.where(below_diag, w, 0.0)
            acc = acc + jnp.dot(w.astype(BF16), v, preferred_element_type=F32)
            return acc, run + sums[:, blk:]

        zero = jnp.zeros((blk, LANES), F32)
        carry = block(qi, (zero, zero), diagonal=True)
        acc, _ = lax.fori_loop(0, qi, lambda j, c: block(qi - 1 - j, c, diagonal=False), carry)
        return acc

    lo_mask, hi_mask = _head_lane_masks()
    o_ref[...] = jnp.where(lo_mask, head(lo_mask), head(hi_mask)).astype(o_ref.dtype)


def _sb_attention(q, k, v):
    batch, seq, _ = q.shape
    kv_spec = pl.BlockSpec((None, seq, LANES), lambda b, p, i: (b, 0, p))
    q_spec = pl.BlockSpec((None, SB_BLOCK, LANES), lambda b, p, i: (b, i, p))
    return pl.pallas_call(
        _sb_kernel,
        grid=(batch, HEAD_PAIRS, seq // SB_BLOCK),
        in_specs=[q_spec, kv_spec, kv_spec],
        out_specs=q_spec,
        out_shape=jax.ShapeDtypeStruct(q.shape, BF16),
        compiler_params=pltpu.CompilerParams(
            dimension_semantics=("parallel", "parallel", "arbitrary"),
            vmem_limit_bytes=VMEM_LIMIT),
        name="stick_breaking_attention",
    )(q, k, v)


def _moba_kernel(q_ref, k_ref, v_ref, o_ref, kmean_ref, *, n_blocks):
    qi = pl.program_id(2)
    blk = MOBA_BLOCK

    @pl.when(qi == 0)
    def _():
        for n in range(n_blocks):
            kn = k_ref[n * blk:(n + 1) * blk, :].astype(F32)
            kmean_ref[n:n + 1, :] = jnp.mean(kn, axis=0, keepdims=True)

    q = q_ref[...]
    kmean = kmean_ref[...].astype(BF16)
    row = lax.broadcasted_iota(jnp.int32, (blk, blk), 0)
    col = lax.broadcasted_iota(jnp.int32, (blk, blk), 1)
    causal = col <= row
    blk_id = lax.broadcasted_iota(jnp.int32, (1, n_blocks), 1)
    past = blk_id < qi

    def head(lane_mask):
        qh = jnp.where(lane_mask, q, jnp.zeros_like(q))
        gate = lax.dot_general(qh, kmean, _NT, preferred_element_type=F32)
        selected = jnp.zeros((blk, n_blocks), F32)
        for n in range(n_blocks):
            g_n = gate[:, n:n + 1]
            beats = jnp.logical_and(
                past, jnp.logical_or(gate > g_n, jnp.logical_and(gate == g_n, blk_id < n)))
            rank = jnp.sum(beats.astype(F32), axis=-1, keepdims=True)
            pick = jnp.logical_and(rank < MOBA_TOPK, n < qi)
            selected = jnp.where(jnp.logical_and(blk_id == n, pick), 1.0, selected)

        def block(kb, carry, diagonal):
            m, l, acc = carry
            start = pl.multiple_of(kb * blk, blk)
            k = k_ref[pl.ds(start, blk), :]
            v = v_ref[pl.ds(start, blk), :]
            s = lax.dot_general(qh, k, _NT, preferred_element_type=F32)
            if diagonal:
                mask = causal
            else:
                mask = jnp.sum(jnp.where(blk_id == kb, selected, 0.0), axis=-1, keepdims=True) > 0.5
            s = jnp.where(mask, s, -jnp.inf)
            m_new = jnp.maximum(m, jnp.max(s, axis=-1, keepdims=True))
            p = jnp.exp(s - m_new)
            alpha = jnp.exp(m - m_new)
            l = alpha * l + jnp.sum(p, axis=-1, keepdims=True)
            acc = alpha * acc + jnp.dot(p.astype(BF16), v, preferred_element_type=F32)
            return m_new, l, acc

        m0 = jnp.full((blk, 1), -jnp.inf, F32)
        l0 = jnp.zeros((blk, 1), F32)
        acc0 = jnp.zeros((blk, LANES), F32)
        carry = block(qi, (m0, l0, acc0), diagonal=True)
        _, l, acc = lax.fori_loop(0, qi, lambda j, c: block(qi - 1 - j, c, diagonal=False), carry)
        return acc / l

    lo_mask, hi_mask = _head_lane_masks()
    o_ref[...] = jnp.where(lo_mask, head(lo_mask), head(hi_mask)).astype(o_ref.dtype)


def _moba_attention(q, k, v):
    batch, seq, _ = q.shape
    n_blocks = seq // MOBA_BLOCK
    kv_spec = pl.BlockSpec((None, seq, LANES), lambda b, p, i: (b, 0, p))
    q_spec = pl.BlockSpec((None, MOBA_BLOCK, LANES), lambda b, p, i: (b, i, p))
    return pl.pallas_call(
        functools.partial(_moba_kernel, n_blocks=n_blocks),
        grid=(batch, HEAD_PAIRS, n_blocks),
        in_specs=[q_spec, kv_spec, kv_spec],
        out_specs=q_spec,
        out_shape=jax.ShapeDtypeStruct(q.shape, BF16),
        scratch_shapes=[pltpu.VMEM((n_blocks, LANES), F32)],
        compiler_params=pltpu.CompilerParams(
            dimension_semantics=("parallel", "parallel", "arbitrary"),
            vmem_limit_bytes=VMEM_LIMIT),
        name="moba_attention",
    )(q, k, v)


def _attn_out_kernel(x_ref, mod_ref, oa_ref, ob_ref, ga_ref, gb_ref, w_ref, o_ref):
    gate = mod_ref[:, 2 * D_MODEL:3 * D_MODEL]
    ya = (oa_ref[...].astype(F32) * ga_ref[...].astype(F32)).astype(BF16)
    yb = (ob_ref[...].astype(F32) * gb_ref[...].astype(F32)).astype(BF16)
    y = (jnp.dot(ya, w_ref[0:GROUP_WIDTH, :], preferred_element_type=F32)
         + jnp.dot(yb, w_ref[GROUP_WIDTH:2 * GROUP_WIDTH, :], preferred_element_type=F32))
    o_ref[...] = x_ref[...] + gate * y


def _attn_output(x, mod, o_a, o_b, g_a, g_b, w_out):
    batch, seq, _ = x.shape
    row_spec = pl.BlockSpec((None, PROJ_ROWS, D_MODEL), lambda b, i: (b, i, 0))
    half_spec = pl.BlockSpec((None, PROJ_ROWS, GROUP_WIDTH), lambda b, i: (b, i, 0))
    return pl.pallas_call(
        _attn_out_kernel,
        grid=(batch, seq // PROJ_ROWS),
        in_specs=[
            row_spec,
            pl.BlockSpec((None, None, 1, 3 * D_MODEL), lambda b, i: (0, b, 0, 0)),
            half_spec, half_spec, half_spec, half_spec,
            _resident((D_MODEL, D_MODEL), lambda b, i: (0, 0)),
        ],
        out_specs=row_spec,
        out_shape=jax.ShapeDtypeStruct(x.shape, F32),
        compiler_params=pltpu.CompilerParams(
            dimension_semantics=("parallel", "parallel"), vmem_limit_bytes=VMEM_LIMIT),
        name="attn_out_projection",
    )(x, mod.reshape(DEPTH, batch, 1, 3 * D_MODEL), o_a, o_b, g_a, g_b, w_out.astype(BF16))


def _lru_kernel(x_ref, mod_ref, g_ref, w_in_ref, conv_w_ref, conv_b_ref, w_gate_ref,
                b_a_ref, b_x_ref, lam_ref, w_out_ref, final_g_ref, o_ref,
                h_ref, xb_ref, xc_ref, sg_ref, a_ref, b_ref, y_ref, state_ref):
    step_i = pl.program_id(1)
    nb, nt, pitch = LRU_BATCH, LRU_STEPS, LRU_PITCH
    width = LRU_BLOCK_WIDTH

    @pl.when(step_i == 0)
    def _():
        state_ref[...] = jnp.zeros_like(state_ref)
        xb_ref[:, 0:8, :] = jnp.zeros((nb, 8, D_MODEL), F32)

    for b in range(nb):
        shift = mod_ref[b, :, 0:D_MODEL]
        scale = mod_ref[b, :, D_MODEL:2 * D_MODEL]
        h = _rms(x_ref[b], g_ref[...]) * (1.0 + scale) + shift
        h_ref[b * nt:(b + 1) * nt, :] = h.astype(BF16)

    xb = jnp.dot(h_ref[...], w_in_ref[:, 0:D_MODEL], preferred_element_type=F32)
    for b in range(nb):
        xb_ref[b, 8:8 + nt, :] = xb[b * nt:(b + 1) * nt, :]
    gate_branch = jnp.dot(h_ref[...], w_in_ref[:, D_MODEL:2 * D_MODEL], preferred_element_type=F32)
    sg_ref[...] = jax.nn.silu(gate_branch)

    for b in range(nb):
        xc = conv_b_ref[...] + conv_w_ref[CONV_WIDTH - 1:CONV_WIDTH, :] * xb_ref[b, 8:8 + nt, :]
        for tap in range(CONV_WIDTH - 1):
            off = 8 - (CONV_WIDTH - 1) + tap
            xc = xc + conv_w_ref[tap:tap + 1, :] * xb_ref[b, off:off + nt, :]
        xc_ref[b * nt:(b + 1) * nt, :] = xc
        xb_ref[b, 0:8, :] = xb_ref[b, nt:nt + 8, :]

    for n in range(LRU_BLOCKS):
        cols = slice(n * width, (n + 1) * width)
        xc_n = xc_ref[:, cols]
        gates = jnp.dot(xc_n.astype(BF16), w_gate_ref[n], preferred_element_type=F32)
        r = jax.nn.sigmoid(gates[:, 0:width] + b_a_ref[:, cols])
        inp = jax.nn.sigmoid(gates[:, width:2 * width] + b_x_ref[:, cols])
        log_a = LRU_C * r * jax.nn.log_sigmoid(lam_ref[:, cols])
        a = jnp.exp(log_a)
        mult = jnp.sqrt(-jnp.tanh(log_a) * (a * a + 1.0))
        b_term = mult * (inp * xc_n)
        for b in range(nb):
            a_ref[n, b * pitch:b * pitch + nt, :] = a[b * nt:(b + 1) * nt, :]
            b_ref[n, b * pitch:b * pitch + nt, :] = b_term[b * nt:(b + 1) * nt, :]

    def scan_step(t, hs):
        new = []
        for n in range(LRU_BLOCKS):
            rows = pl.ds(t, nb, stride=pitch)
            h_n = a_ref[n, rows, :] * hs[n] + b_ref[n, rows, :]
            b_ref[n, rows, :] = h_n
            new.append(h_n)
        return tuple(new)

    hs = lax.fori_loop(0, nt, scan_step, tuple(state_ref[n] for n in range(LRU_BLOCKS)))
    for n in range(LRU_BLOCKS):
        state_ref[n] = hs[n]

    for b in range(nb):
        hs_b = jnp.concatenate(
            [b_ref[n, b * pitch:b * pitch + nt, :] for n in range(LRU_BLOCKS)], axis=1)
        y_ref[b * nt:(b + 1) * nt, :] = (hs_b * sg_ref[b * nt:(b + 1) * nt, :]).astype(BF16)

    y = jnp.dot(y_ref[...], w_out_ref[...], preferred_element_type=F32)
    for b in range(nb):
        gate = mod_ref[b, :, 2 * D_MODEL:3 * D_MODEL]
        x_new = x_ref[b] + gate * y[b * nt:(b + 1) * nt, :]
        o_ref[b] = _rms(x_new, final_g_ref[...])


def _lru_layer(x, mod, norm_g, w_in, conv_w, conv_b, w_a, b_a, w_x, b_x, lam, w_out, final_g):
    batch, seq, _ = x.shape
    rows = LRU_BATCH * LRU_STEPS
    x_spec = pl.BlockSpec((LRU_BATCH, LRU_STEPS, D_MODEL), lambda g, i: (g, i, 0))
    vec_spec = pl.BlockSpec((1, D_MODEL), lambda g, i: (0, 0))
    w_gate = jnp.concatenate([w_a, w_x], axis=-1).astype(BF16)
    return pl.pallas_call(
        _lru_kernel,
        grid=(batch // LRU_BATCH, seq // LRU_STEPS),
        in_specs=[
            x_spec,
            pl.BlockSpec((None, LRU_BATCH, 1, 3 * D_MODEL), lambda g, i: (1, g, 0, 0)),
            pl.BlockSpec((None, 1, D_MODEL), lambda g, i: (1, 0, 0)),
            _resident((D_MODEL, 2 * D_MODEL), lambda g, i: (0, 0)),
            pl.BlockSpec((CONV_WIDTH, D_MODEL), lambda g, i: (0, 0)),
            vec_spec,
            _resident((LRU_BLOCKS, LRU_BLOCK_WIDTH, 2 * LRU_BLOCK_WIDTH), lambda g, i: (0, 0, 0)),
            vec_spec, vec_spec, vec_spec,
            _resident((D_MODEL, D_MODEL), lambda g, i: (0, 0)),
            vec_spec,
        ],
        out_specs=x_spec,
        out_shape=jax.ShapeDtypeStruct(x.shape, F32),
        scratch_shapes=[
            pltpu.VMEM((rows, D_MODEL), BF16),
            pltpu.VMEM((LRU_BATCH, 8 + LRU_STEPS, D_MODEL), F32),
            pltpu.VMEM((rows, D_MODEL), F32),
            pltpu.VMEM((rows, D_MODEL), F32),
            pltpu.VMEM((LRU_BLOCKS, LRU_BATCH * LRU_PITCH, LRU_BLOCK_WIDTH), F32),
            pltpu.VMEM((LRU_BLOCKS, LRU_BATCH * LRU_PITCH, LRU_BLOCK_WIDTH), F32),
            pltpu.VMEM((rows, D_MODEL), BF16),
            pltpu.VMEM((LRU_BLOCKS, LRU_BATCH, LRU_BLOCK_WIDTH), F32),
        ],
        compiler_params=pltpu.CompilerParams(
            dimension_semantics=("parallel", "arbitrary"), vmem_limit_bytes=VMEM_LIMIT),
        name="rglru_layer_final_norm",
    )(x, mod.reshape(DEPTH, batch, 1, 3 * D_MODEL), norm_g.reshape(DEPTH, 1, D_MODEL),
      w_in.astype(BF16), conv_w, conv_b.reshape(1, D_MODEL), w_gate,
      b_a.reshape(1, D_MODEL), b_x.reshape(1, D_MODEL), lam.reshape(1, D_MODEL),
      w_out.astype(BF16), final_g.reshape(1, D_MODEL))


def kernel(x, c, positions, norm_g, w_mod, b_mod, attn_w_in, attn_w_out, lru_w_in, lru_conv_w,
           lru_conv_b, lru_w_a, lru_b_a, lru_w_x, lru_b_x, lru_lambda, lru_w_out, final_g):
    batch, seq, d_model = x.shape
    assert d_model == D_MODEL and DEPTH == 2
    assert batch % LRU_BATCH == 0 and seq % PROJ_ROWS == 0 and seq % MOBA_BLOCK == 0
    mod = _modulation(c, w_mod, b_mod)
    q_a, k_a, v_a, q_b, k_b, v_b, g_a, g_b = _attn_projection(x, mod, norm_g, positions, attn_w_in[0])
    o_a = _sb_attention(q_a, k_a, v_a)
    o_b = _moba_attention(q_b, k_b, v_b)
    x = _attn_output(x, mod, o_a, o_b, g_a, g_b, attn_w_out[0])
    return _lru_layer(x, mod, norm_g, lru_w_in[0], lru_conv_w[0], lru_conv_b[0], lru_w_a[0],
                      lru_b_a[0], lru_w_x[0], lru_b_x[0], lru_lambda[0], lru_w_out[0], final_g)
```

```python
import functools

import jax
import jax.numpy as jnp
import numpy as np
from jax import lax
from jax.experimental import pallas as pl
from jax.experimental.pallas import tpu as pltpu

F32 = jnp.float32
BF16 = jnp.bfloat16

D_MODEL = 1024
DEPTH = 2
HEAD_DIM = 64
N_HEADS = 8
GROUP_WIDTH = N_HEADS * HEAD_DIM
LANES = 128
HEAD_PAIRS = GROUP_WIDTH // LANES
SB_BLOCK = 128
MOBA_BLOCK = 256
MOBA_TOPK = 3
ROPE_THETA = 500000.0
ROT_DIM = HEAD_DIM // 4
LRU_BLOCKS = 8
LRU_BLOCK_WIDTH = D_MODEL // LRU_BLOCKS
CONV_WIDTH = 4
LRU_C = 8.0
EPS = 1e-6

PROJ_ROWS = 512
LRU_BATCH = 8
LRU_STEPS = 64
LRU_PITCH = LRU_STEPS + 8
VMEM_LIMIT = 56 * 1024 * 1024
F32_EXP_UNDERFLOW = -104.0

_NT = (((1,), (1,)), ((), ()))


def _resident(shape, index_map):
    return pl.BlockSpec(shape, index_map, pipeline_mode=pl.Buffered(1))


def _rms(x, g):
    ms = jnp.mean(x * x, axis=-1, keepdims=True)
    return x * lax.rsqrt(ms + EPS) * g


def _mod_kernel(c_ref, w_ref, b_ref, o_ref):
    o_ref[...] = jnp.dot(c_ref[...], w_ref[...], preferred_element_type=F32,
                         precision=lax.Precision.HIGHEST) + b_ref[...]


def _modulation(c, w_mod, b_mod):
    batch = c.shape[0]
    return pl.pallas_call(
        _mod_kernel,
        grid=(DEPTH, 3),
        in_specs=[
            pl.BlockSpec((batch, D_MODEL), lambda l, j: (0, 0)),
            pl.BlockSpec((None, D_MODEL, D_MODEL), lambda l, j: (l, 0, j)),
            pl.BlockSpec((None, 1, D_MODEL), lambda l, j: (l, 0, j)),
        ],
        out_specs=pl.BlockSpec((None, batch, D_MODEL), lambda l, j: (l, 0, j)),
        out_shape=jax.ShapeDtypeStruct((DEPTH, batch, 3 * D_MODEL), F32),
        name="adaln_modulation",
    )(c, w_mod, b_mod.reshape(DEPTH, 1, 3 * D_MODEL))


def _proj_kernel(x_ref, mod_ref, g_ref, pos_ref, inv_ref, w_ref, wvt_ref,
                 qa_ref, ka_ref, va_ref, qb_ref, kb_ref, vbt_ref, ga_ref, gb_ref):
    shift = mod_ref[:, 0:D_MODEL]
    scale = mod_ref[:, D_MODEL:2 * D_MODEL]
    h = (_rms(x_ref[...], g_ref[...]) * (1.0 + scale) + shift).astype(BF16)

    def proj(j):
        return jnp.dot(h, w_ref[:, j * GROUP_WIDTH:(j + 1) * GROUP_WIDTH],
                       preferred_element_type=F32)

    qk_scale = HEAD_DIM ** -0.5
    qa_ref[...] = (proj(0) * qk_scale).astype(BF16)
    ka_ref[...] = proj(1).astype(BF16)
    va_ref[...] = proj(2).astype(BF16)
    vbt_ref[...] = lax.dot_general(wvt_ref[...], h, _NT, preferred_element_type=F32).astype(BF16)
    ga_ref[...] = jax.nn.silu(proj(5)).astype(BF16)
    gb_ref[...] = jax.nn.silu(proj(6)).astype(BF16)
    ang = pos_ref[...].astype(F32) * inv_ref[...]
    cos, sin = jnp.cos(ang), jnp.sin(ang)
    qb_ref[...] = ((proj(3) * cos + proj(7) * sin) * qk_scale).astype(BF16)
    kb_ref[...] = (proj(4) * cos + proj(8) * sin).astype(BF16)


def _rope_partner(w):
    w = w.reshape(D_MODEL, N_HEADS, HEAD_DIM)
    half = ROT_DIM // 2
    part = jnp.zeros_like(w)
    part = part.at[:, :, :half].set(-w[:, :, half:ROT_DIM])
    part = part.at[:, :, half:ROT_DIM].set(w[:, :, :half])
    return part.reshape(D_MODEL, GROUP_WIDTH)


def _rope_inv_lanes():
    half = ROT_DIM // 2
    inv = ROPE_THETA ** (-np.arange(0, ROT_DIM, 2, dtype=np.float32) / ROT_DIM)
    per_head = np.zeros((HEAD_DIM,), np.float32)
    per_head[:half] = inv
    per_head[half:ROT_DIM] = inv
    return jnp.asarray(np.tile(per_head, N_HEADS).reshape(1, GROUP_WIDTH))


def _attn_projection(x, mod, norm_g, positions, w_in):
    batch, seq, _ = x.shape
    gw = GROUP_WIDTH
    cols = lambda j: w_in[:, j * gw:(j + 1) * gw]
    w_all = jnp.concatenate(
        [cols(0), cols(1), cols(2), cols(3), cols(4), cols(6), cols(7),
         _rope_partner(cols(3)), _rope_partner(cols(4))], axis=1).astype(BF16)
    w_vbt = cols(5).T.astype(BF16)
    n_cols = w_all.shape[1]
    out_spec = pl.BlockSpec((None, PROJ_ROWS, gw), lambda b, i: (b, i, 0))
    out_shape = jax.ShapeDtypeStruct((batch, seq, gw), BF16)
    out_specs = [out_spec] * 8
    out_shapes = [out_shape] * 8
    out_specs[5] = pl.BlockSpec((None, gw, PROJ_ROWS), lambda b, i: (b, 0, i))
    out_shapes[5] = jax.ShapeDtypeStruct((batch, gw, seq), BF16)
    return pl.pallas_call(
        _proj_kernel,
        grid=(batch, seq // PROJ_ROWS),
        in_specs=[
            pl.BlockSpec((None, PROJ_ROWS, D_MODEL), lambda b, i: (b, i, 0)),
            pl.BlockSpec((None, None, 1, 3 * D_MODEL), lambda b, i: (0, b, 0, 0)),
            pl.BlockSpec((None, 1, D_MODEL), lambda b, i: (0, 0, 0)),
            pl.BlockSpec((None, PROJ_ROWS, 1), lambda b, i: (b, i, 0)),
            pl.BlockSpec((1, gw), lambda b, i: (0, 0)),
            _resident((D_MODEL, n_cols), lambda b, i: (0, 0)),
            _resident((gw, D_MODEL), lambda b, i: (0, 0)),
        ],
        out_specs=out_specs,
        out_shape=out_shapes,
        compiler_params=pltpu.CompilerParams(
            dimension_semantics=("parallel", "parallel"), vmem_limit_bytes=VMEM_LIMIT),
        name="attn_in_projection",
    )(x, mod.reshape(DEPTH, batch, 1, 3 * D_MODEL), norm_g.reshape(DEPTH, 1, D_MODEL),
      positions.reshape(batch, seq, 1), _rope_inv_lanes(), w_all, w_vbt)


def _head_lane_masks():
    lane = lax.broadcasted_iota(jnp.int32, (1, LANES), 1)
    return lane < HEAD_DIM, lane >= HEAD_DIM


def _split_heads_rows(x, lo_mask, hi_mask):
    zero = jnp.zeros_like(x)
    return jnp.concatenate([jnp.where(lo_mask, x, zero), jnp.where(hi_mask, x, zero)], axis=0)


def _sb_kernel(q_ref, k_ref, v_ref, o_ref, z_ref, later_ref, total_ref, run_ref, acc_ref):
    qi = pl.program_id(1)
    blk = SB_BLOCK
    lo_mask, hi_mask = _head_lane_masks()
    row = lax.broadcasted_iota(jnp.int32, (blk, 2 * blk), 0)
    col = lax.broadcasted_iota(jnp.int32, (blk, 2 * blk), 1)
    below_diag = jnp.where(col < blk, col, col - blk) < row
    r = lax.broadcasted_iota(jnp.int32, (blk, blk), 0)
    c = lax.broadcasted_iota(jnp.int32, (blk, blk), 1)
    later_and_total = jnp.concatenate([(r > c).astype(BF16), jnp.ones((blk, blk), BF16)], axis=1)
    later_and_total = jnp.concatenate([later_and_total, later_and_total], axis=0)
    q_pairs = [q_ref[:, p * LANES:(p + 1) * LANES] for p in range(HEAD_PAIRS)]

    def sweep(kb, diagonal):
        start = pl.multiple_of(kb * blk, blk)
        for p in range(HEAD_PAIRS):
            k2 = _split_heads_rows(k_ref[pl.ds(start, blk), p * LANES:(p + 1) * LANES], lo_mask, hi_mask)
            z_ref[p] = lax.dot_general(q_pairs[p], k2, _NT, preferred_element_type=F32)
        for p in range(HEAD_PAIRS):
            z = z_ref[p]
            lp = jnp.log(1.0 + jnp.exp(-jnp.abs(z)))
            log_1m = jnp.minimum(-z, 0.0) - lp
            z_ref[p] = jnp.minimum(z, 0.0) - lp
            if diagonal:
                log_1m = jnp.where(below_diag, log_1m, 0.0)
            hi = log_1m.astype(BF16)
            lo = (log_1m - hi.astype(F32)).astype(BF16)
            for h in range(2):
                split = jnp.concatenate([hi[:, h * blk:(h + 1) * blk], lo[:, h * blk:(h + 1) * blk]], axis=1)
                sums = jnp.dot(split, later_and_total, preferred_element_type=F32)
                later_ref[p, :, h * blk:(h + 1) * blk] = sums[:, :blk]
                total_ref[p, :, h * blk:(h + 1) * blk] = sums[:, blk:]
        worst = None
        for p in range(HEAD_PAIRS):
            run = run_ref[p]
            w = jnp.exp(z_ref[p] + later_ref[p] + run)
            if diagonal:
                w = jnp.where(below_diag, w, 0.0)
            v2 = _split_heads_rows(v_ref[pl.ds(start, blk), p * LANES:(p + 1) * LANES], lo_mask, hi_mask)
            acc_ref[p] += jnp.dot(w.astype(BF16), v2, preferred_element_type=F32)
            run = run + total_ref[p]
            run_ref[p] = run
            worst = run if worst is None else jnp.maximum(worst, run)
        return jnp.max(worst)

    acc_ref[...] = jnp.zeros_like(acc_ref)
    run_ref[...] = jnp.zeros_like(run_ref)
    worst = sweep(qi, diagonal=True)

    def more(carry):
        j, worst = carry
        return jnp.logical_and(j < qi, worst > F32_EXP_UNDERFLOW)

    def past_block(carry):
        j, _ = carry
        return j + 1, sweep(qi - 1 - j, diagonal=False)

    lax.while_loop(more, past_block, (jnp.int32(0), worst))
    for p in range(HEAD_PAIRS):
        o_ref[:, p * LANES:(p + 1) * LANES] = acc_ref[p].astype(o_ref.dtype)


def _sb_attention(q, k, v):
    batch, seq, width = q.shape
    kv_spec = pl.BlockSpec((None, seq, width), lambda b, i: (b, 0, 0))
    q_spec = pl.BlockSpec((None, SB_BLOCK, width), lambda b, i: (b, i, 0))
    return pl.pallas_call(
        _sb_kernel,
        grid=(batch, seq // SB_BLOCK),
        in_specs=[q_spec, kv_spec, kv_spec],
        out_specs=q_spec,
        out_shape=jax.ShapeDtypeStruct(q.shape, BF16),
        scratch_shapes=[pltpu.VMEM((HEAD_PAIRS, SB_BLOCK, 2 * SB_BLOCK), F32)] * 4
        + [pltpu.VMEM((HEAD_PAIRS, SB_BLOCK, LANES), F32)],
        compiler_params=pltpu.CompilerParams(
            dimension_semantics=("parallel", "arbitrary"), vmem_limit_bytes=VMEM_LIMIT),
        name="stick_breaking_attention",
    )(q, k, v)


def _moba_kernel(q_ref, k_ref, vt_ref, o_ref, kmean_ref, sel_ref, s_ref, *, n_blocks):
    qi = pl.program_id(1)
    blk = MOBA_BLOCK
    ones_rows = 16

    @pl.when(qi == 0)
    def _():
        for n in range(n_blocks):
            kn = k_ref[n * blk:(n + 1) * blk, :].astype(F32)
            kmean_ref[n:n + 1, :] = jnp.mean(kn, axis=0, keepdims=True)

    lo_mask, hi_mask = _head_lane_masks()
    blk_row = lax.broadcasted_iota(jnp.int32, (n_blocks, blk), 0)
    blk_row_f = blk_row.astype(F32)
    past = blk_row < qi
    key_row = lax.broadcasted_iota(jnp.int32, (blk, blk), 0)
    query_col = lax.broadcasted_iota(jnp.int32, (blk, blk), 1)
    causal = key_row <= query_col

    q_heads = []
    for p in range(HEAD_PAIRS):
        q_pair = q_ref[:, p * LANES:(p + 1) * LANES]
        kmean = kmean_ref[:, p * LANES:(p + 1) * LANES].astype(BF16)
        for mask in (lo_mask, hi_mask):
            qh = jnp.where(mask, q_pair, jnp.zeros_like(q_pair))
            gate = lax.dot_general(kmean, qh, _NT, preferred_element_type=F32)
            gate = jnp.where(past, gate, -jnp.inf)
            selected = jnp.zeros((n_blocks, blk), F32)
            for _ in range(MOBA_TOPK):
                best = jnp.max(gate, axis=0, keepdims=True)
                first = jnp.min(jnp.where(gate == best, blk_row_f, float(n_blocks)), axis=0, keepdims=True)
                hit = blk_row_f == first
                selected = jnp.where(jnp.logical_and(hit, past), 1.0, selected)
                gate = jnp.where(hit, -jnp.inf, gate)
            sel_ref[len(q_heads)] = selected
            q_heads.append(qh)

    def head_block(h, kb, state, vt_pair, diagonal):
        m, acc = state
        s = s_ref[h]
        if diagonal:
            s = jnp.where(causal, s, -jnp.inf)
        else:
            s = s + jnp.where(sel_ref[h, pl.ds(kb, 1), :] > 0.5, 0.0, -jnp.inf)
        m_new = jnp.maximum(m, jnp.max(s, axis=0, keepdims=True))
        p = jnp.exp(s - m_new).astype(BF16)
        alpha = jnp.exp(m - m_new)
        sub = (h % 2) * HEAD_DIM
        vt_aug = jnp.concatenate([vt_pair[sub:sub + HEAD_DIM, :], jnp.ones((ones_rows, blk), BF16)], axis=0)
        acc = alpha * acc + jnp.dot(vt_aug, p, preferred_element_type=F32)
        return m_new, acc

    def sweep(kb, states, diagonal):
        start = pl.multiple_of(kb * blk, blk)
        for p in range(HEAD_PAIRS):
            k_pair = k_ref[pl.ds(start, blk), p * LANES:(p + 1) * LANES]
            for h in (2 * p, 2 * p + 1):
                s_ref[h] = lax.dot_general(k_pair, q_heads[h], _NT, preferred_element_type=F32)
        new = []
        for p in range(HEAD_PAIRS):
            vt_pair = vt_ref[p * LANES:(p + 1) * LANES, pl.ds(start, blk)]
            for h in (2 * p, 2 * p + 1):
                new.append(head_block(h, kb, states[h], vt_pair, diagonal))
        return tuple(new)

    init = (jnp.full((1, blk), -jnp.inf, F32), jnp.zeros((HEAD_DIM + ones_rows, blk), F32))
    states = sweep(qi, (init,) * N_HEADS, diagonal=True)
    states = lax.fori_loop(0, qi, lambda j, st: sweep(qi - 1 - j, st, diagonal=False), states)
    for p in range(HEAD_PAIRS):
        outs = []
        for h in (2 * p, 2 * p + 1):
            acc = states[h][1]
            outs.append(acc[0:HEAD_DIM, :] / acc[HEAD_DIM:HEAD_DIM + 1, :])
        o_ref[:, p * LANES:(p + 1) * LANES] = jnp.concatenate(outs, axis=0).T.astype(o_ref.dtype)


def _moba_attention(q, k, vt):
    batch, seq, width = q.shape
    n_blocks = seq // MOBA_BLOCK
    q_spec = pl.BlockSpec((None, MOBA_BLOCK, width), lambda b, i: (b, i, 0))
    return pl.pallas_call(
        functools.partial(_moba_kernel, n_blocks=n_blocks),
        grid=(batch, n_blocks),
        in_specs=[q_spec,
                  pl.BlockSpec((None, seq, width), lambda b, i: (b, 0, 0)),
                  pl.BlockSpec((None, width, seq), lambda b, i: (b, 0, 0))],
        out_specs=q_spec,
        out_shape=jax.ShapeDtypeStruct(q.shape, BF16),
        scratch_shapes=[pltpu.VMEM((n_blocks, width), F32),
                        pltpu.VMEM((N_HEADS, n_blocks, MOBA_BLOCK), F32),
                        pltpu.VMEM((N_HEADS, MOBA_BLOCK, MOBA_BLOCK), F32)],
        compiler_params=pltpu.CompilerParams(
            dimension_semantics=("parallel", "arbitrary"), vmem_limit_bytes=VMEM_LIMIT),
        name="moba_attention",
    )(q, k, vt)


def _attn_out_kernel(x_ref, mod_ref, oa_ref, ob_ref, ga_ref, gb_ref, w_ref, o_ref):
    gate = mod_ref[:, 2 * D_MODEL:3 * D_MODEL]
    ya = (oa_ref[...].astype(F32) * ga_ref[...].astype(F32)).astype(BF16)
    yb = (ob_ref[...].astype(F32) * gb_ref[...].astype(F32)).astype(BF16)
    y = (jnp.dot(ya, w_ref[0:GROUP_WIDTH, :], preferred_element_type=F32)
         + jnp.dot(yb, w_ref[GROUP_WIDTH:2 * GROUP_WIDTH, :], preferred_element_type=F32))
    o_ref[...] = x_ref[...] + gate * y


def _attn_output(x, mod, o_a, o_b, g_a, g_b, w_out):
    batch, seq, _ = x.shape
    row_spec = pl.BlockSpec((None, PROJ_ROWS, D_MODEL), lambda b, i: (b, i, 0))
    half_spec = pl.BlockSpec((None, PROJ_ROWS, GROUP_WIDTH), lambda b, i: (b, i, 0))
    return pl.pallas_call(
        _attn_out_kernel,
        grid=(batch, seq // PROJ_ROWS),
        in_specs=[
            row_spec,
            pl.BlockSpec((None, None, 1, 3 * D_MODEL), lambda b, i: (0, b, 0, 0)),
            half_spec, half_spec, half_spec, half_spec,
            _resident((D_MODEL, D_MODEL), lambda b, i: (0, 0)),
        ],
        out_specs=row_spec,
        out_shape=jax.ShapeDtypeStruct(x.shape, F32),
        compiler_params=pltpu.CompilerParams(
            dimension_semantics=("parallel", "parallel"), vmem_limit_bytes=VMEM_LIMIT),
        name="attn_out_projection",
    )(x, mod.reshape(DEPTH, batch, 1, 3 * D_MODEL), o_a, o_b, g_a, g_b, w_out.astype(BF16))


def _lru_kernel(x_ref, mod_ref, g_ref, w_in_ref, conv_w_ref, conv_b_ref, w_gate_ref,
                b_a_ref, b_x_ref, lam_ref, w_out_ref, final_g_ref, o_ref,
                h_ref, xb_ref, xc_ref, sg_ref, a_ref, b_ref, y_ref, state_ref):
    step_i = pl.program_id(1)
    nb, nt, pitch = LRU_BATCH, LRU_STEPS, LRU_PITCH
    width = LRU_BLOCK_WIDTH

    @pl.when(step_i == 0)
    def _():
        state_ref[...] = jnp.zeros_like(state_ref)
        xb_ref[:, 0:8, :] = jnp.zeros((nb, 8, D_MODEL), F32)

    for b in range(nb):
        shift = mod_ref[b, :, 0:D_MODEL]
        scale = mod_ref[b, :, D_MODEL:2 * D_MODEL]
        h = _rms(x_ref[b], g_ref[...]) * (1.0 + scale) + shift
        h_ref[b * nt:(b + 1) * nt, :] = h.astype(BF16)

    xb = jnp.dot(h_ref[...], w_in_ref[:, 0:D_MODEL], preferred_element_type=F32)
    for b in range(nb):
        xb_ref[b, 8:8 + nt, :] = xb[b * nt:(b + 1) * nt, :]
    gate_branch = jnp.dot(h_ref[...], w_in_ref[:, D_MODEL:2 * D_MODEL], preferred_element_type=F32)
    sg_ref[...] = jax.nn.silu(gate_branch)

    for b in range(nb):
        xc = conv_b_ref[...] + conv_w_ref[CONV_WIDTH - 1:CONV_WIDTH, :] * xb_ref[b, 8:8 + nt, :]
        for tap in range(CONV_WIDTH - 1):
            off = 8 - (CONV_WIDTH - 1) + tap
            xc = xc + conv_w_ref[tap:tap + 1, :] * xb_ref[b, off:off + nt, :]
        xc_ref[b * nt:(b + 1) * nt, :] = xc
        xb_ref[b, 0:8, :] = xb_ref[b, nt:nt + 8, :]

    for n in range(LRU_BLOCKS):
        cols = slice(n * width, (n + 1) * width)
        xc_n = xc_ref[:, cols]
        gates = jnp.dot(xc_n.astype(BF16), w_gate_ref[n], preferred_element_type=F32)
        r = jax.nn.sigmoid(gates[:, 0:width] + b_a_ref[:, cols])
        inp = jax.nn.sigmoid(gates[:, width:2 * width] + b_x_ref[:, cols])
        log_a = LRU_C * r * jax.nn.log_sigmoid(lam_ref[:, cols])
        a = jnp.exp(log_a)
        mult = jnp.sqrt(-jnp.tanh(log_a) * (a * a + 1.0))
        b_term = mult * (inp * xc_n)
        for b in range(nb):
            a_ref[n, b * pitch:b * pitch + nt, :] = a[b * nt:(b + 1) * nt, :]
            b_ref[n, b * pitch:b * pitch + nt, :] = b_term[b * nt:(b + 1) * nt, :]

    def scan_step(t, hs):
        new = []
        for n in range(LRU_BLOCKS):
            rows = pl.ds(t, nb, stride=pitch)
            h_n = a_ref[n, rows, :] * hs[n] + b_ref[n, rows, :]
            b_ref[n, rows, :] = h_n
            new.append(h_n)
        return tuple(new)

    hs = lax.fori_loop(0, nt, scan_step, tuple(state_ref[n] for n in range(LRU_BLOCKS)))
    for n in range(LRU_BLOCKS):
        state_ref[n] = hs[n]

    for b in range(nb):
        hs_b = jnp.concatenate(
            [b_ref[n, b * pitch:b * pitch + nt, :] for n in range(LRU_BLOCKS)], axis=1)
        y_ref[b * nt:(b + 1) * nt, :] = (hs_b * sg_ref[b * nt:(b + 1) * nt, :]).astype(BF16)

    y = jnp.dot(y_ref[...], w_out_ref[...], preferred_element_type=F32)
    for b in range(nb):
        gate = mod_ref[b, :, 2 * D_MODEL:3 * D_MODEL]
        x_new = x_ref[b] + gate * y[b * nt:(b + 1) * nt, :]
        o_ref[b] = _rms(x_new, final_g_ref[...])


def _lru_layer(x, mod, norm_g, w_in, conv_w, conv_b, w_a, b_a, w_x, b_x, lam, w_out, final_g):
    batch, seq, _ = x.shape
    rows = LRU_BATCH * LRU_STEPS
    x_spec = pl.BlockSpec((LRU_BATCH, LRU_STEPS, D_MODEL), lambda g, i: (g, i, 0))
    vec_spec = pl.BlockSpec((1, D_MODEL), lambda g, i: (0, 0))
    w_gate = jnp.concatenate([w_a, w_x], axis=-1).astype(BF16)
    return pl.pallas_call(
        _lru_kernel,
        grid=(batch // LRU_BATCH, seq // LRU_STEPS),
        in_specs=[
            x_spec,
            pl.BlockSpec((None, LRU_BATCH, 1, 3 * D_MODEL), lambda g, i: (1, g, 0, 0)),
            pl.BlockSpec((None, 1, D_MODEL), lambda g, i: (1, 0, 0)),
            _resident((D_MODEL, 2 * D_MODEL), lambda g, i: (0, 0)),
            pl.BlockSpec((CONV_WIDTH, D_MODEL), lambda g, i: (0, 0)),
            vec_spec,
            _resident((LRU_BLOCKS, LRU_BLOCK_WIDTH, 2 * LRU_BLOCK_WIDTH), lambda g, i: (0, 0, 0)),
            vec_spec, vec_spec, vec_spec,
            _resident((D_MODEL, D_MODEL), lambda g, i: (0, 0)),
            vec_spec,
        ],
        out_specs=x_spec,
        out_shape=jax.ShapeDtypeStruct(x.shape, F32),
        scratch_shapes=[
            pltpu.VMEM((rows, D_MODEL), BF16),
            pltpu.VMEM((LRU_BATCH, 8 + LRU_STEPS, D_MODEL), F32),
            pltpu.VMEM((rows, D_MODEL), F32),
            pltpu.VMEM((rows, D_MODEL), F32),
            pltpu.VMEM((LRU_BLOCKS, LRU_BATCH * LRU_PITCH, LRU_BLOCK_WIDTH), F32),
            pltpu.VMEM((LRU_BLOCKS, LRU_BATCH * LRU_PITCH, LRU_BLOCK_WIDTH), F32),
            pltpu.VMEM((rows, D_MODEL), BF16),
            pltpu.VMEM((LRU_BLOCKS, LRU_BATCH, LRU_BLOCK_WIDTH), F32),
        ],
        compiler_params=pltpu.CompilerParams(
            dimension_semantics=("parallel", "arbitrary"), vmem_limit_bytes=VMEM_LIMIT),
        name="rglru_layer_final_norm",
    )(x, mod.reshape(DEPTH, batch, 1, 3 * D_MODEL), norm_g.reshape(DEPTH, 1, D_MODEL),
      w_in.astype(BF16), conv_w, conv_b.reshape(1, D_MODEL), w_gate,
      b_a.reshape(1, D_MODEL), b_x.reshape(1, D_MODEL), lam.reshape(1, D_MODEL),
      w_out.astype(BF16), final_g.reshape(1, D_MODEL))


def kernel(x, c, positions, norm_g, w_mod, b_mod, attn_w_in, attn_w_out, lru_w_in, lru_conv_w,
           lru_conv_b, lru_w_a, lru_b_a, lru_w_x, lru_b_x, lru_lambda, lru_w_out, final_g):
    batch, seq, d_model = x.shape
    assert d_model == D_MODEL and DEPTH == 2
    assert batch % LRU_BATCH == 0 and seq % PROJ_ROWS == 0 and seq % MOBA_BLOCK == 0
    mod = _modulation(c, w_mod, b_mod)
    q_a, k_a, v_a, q_b, k_b, vt_b, g_a, g_b = _attn_projection(x, mod, norm_g, positions, attn_w_in[0])
    o_a = _sb_attention(q_a, k_a, v_a)
    o_b = _moba_attention(q_b, k_b, vt_b)
    x = _attn_output(x, mod, o_a, o_b, g_a, g_b, attn_w_out[0])
    return _lru_layer(x, mod, norm_g, lru_w_in[0], lru_conv_w[0], lru_conv_b[0], lru_w_a[0],
                      lru_b_a[0], lru_w_x[0], lru_b_x[0], lru_lambda[0], lru_w_out[0], final_g)
```

```python
import functools

import jax
import jax.numpy as jnp
import numpy as np
from jax import lax
from jax.experimental import pallas as pl
from jax.experimental.pallas import tpu as pltpu

F32 = jnp.float32
BF16 = jnp.bfloat16

D_MODEL = 1024
DEPTH = 2
HEAD_DIM = 64
N_HEADS = 8
GROUP_WIDTH = N_HEADS * HEAD_DIM
LANES = 128
HEAD_PAIRS = GROUP_WIDTH // LANES
SB_BLOCK = 128
MOBA_BLOCK = 256
MOBA_TOPK = 3
ROPE_THETA = 500000.0
ROT_DIM = HEAD_DIM // 4
LRU_BLOCKS = 8
LRU_BLOCK_WIDTH = D_MODEL // LRU_BLOCKS
CONV_WIDTH = 4
LRU_C = 8.0
EPS = 1e-6

PROJ_ROWS = 512
LRU_BATCH = 8
LRU_STEPS = 64
LRU_PITCH = LRU_STEPS + 8
VMEM_LIMIT = 56 * 1024 * 1024
F32_EXP_UNDERFLOW = -104.0
LOG2_E = 1.4426950408889634

_NT = (((1,), (1,)), ((), ()))


def _resident(shape, index_map):
    return pl.BlockSpec(shape, index_map, pipeline_mode=pl.Buffered(1))


def _rms(x, g):
    ms = jnp.mean(x * x, axis=-1, keepdims=True)
    return x * lax.rsqrt(ms + EPS) * g


def _mod_kernel(c_ref, w_ref, b_ref, o_ref):
    o_ref[...] = jnp.dot(c_ref[...], w_ref[...], preferred_element_type=F32,
                         precision=lax.Precision.HIGHEST) + b_ref[...]


def _modulation(c, w_mod, b_mod):
    batch = c.shape[0]
    return pl.pallas_call(
        _mod_kernel,
        grid=(DEPTH, 3),
        in_specs=[
            pl.BlockSpec((batch, D_MODEL), lambda l, j: (0, 0)),
            pl.BlockSpec((None, D_MODEL, D_MODEL), lambda l, j: (l, 0, j)),
            pl.BlockSpec((None, 1, D_MODEL), lambda l, j: (l, 0, j)),
        ],
        out_specs=pl.BlockSpec((None, batch, D_MODEL), lambda l, j: (l, 0, j)),
        out_shape=jax.ShapeDtypeStruct((DEPTH, batch, 3 * D_MODEL), F32),
        name="adaln_modulation",
    )(c, w_mod, b_mod.reshape(DEPTH, 1, 3 * D_MODEL))


def _proj_kernel(x_ref, mod_ref, g_ref, pos_ref, inv_ref, w_ref, wvt_ref,
                 qa_ref, ka_ref, va_ref, qb_ref, kb_ref, vbt_ref, ga_ref, gb_ref):
    shift = mod_ref[:, 0:D_MODEL]
    scale = mod_ref[:, D_MODEL:2 * D_MODEL]
    h = (_rms(x_ref[...], g_ref[...]) * (1.0 + scale) + shift).astype(BF16)

    def proj(j):
        return jnp.dot(h, w_ref[:, j * GROUP_WIDTH:(j + 1) * GROUP_WIDTH],
                       preferred_element_type=F32)

    qk_scale = HEAD_DIM ** -0.5
    qa_ref[...] = (proj(0) * qk_scale).astype(BF16)
    ka_ref[...] = proj(1).astype(BF16)
    va_ref[...] = proj(2).astype(BF16)
    vbt_ref[...] = lax.dot_general(wvt_ref[...], h, _NT, preferred_element_type=F32).astype(BF16)
    ga_ref[...] = jax.nn.silu(proj(5)).astype(BF16)
    gb_ref[...] = jax.nn.silu(proj(6)).astype(BF16)
    half = ROT_DIM // 2
    ang = pos_ref[...].astype(F32) * inv_ref[...]
    cos, sin = jnp.cos(ang), jnp.sin(ang)
    in_head = lax.broadcasted_iota(jnp.int32, (1, LANES), 1) % HEAD_DIM
    sin_from_below = jnp.where(jnp.logical_and(in_head >= half, in_head < ROT_DIM), sin, 0.0)
    sin_from_above = jnp.where(in_head < half, -sin, 0.0)

    def rotary(t):
        out = []
        for p in range(HEAD_PAIRS):
            tp = t[:, p * LANES:(p + 1) * LANES]
            out.append(tp * cos + pltpu.roll(tp, half, 1) * sin_from_below
                       + pltpu.roll(tp, LANES - half, 1) * sin_from_above)
        return jnp.concatenate(out, axis=1)

    qb_ref[...] = (rotary(proj(3)) * (qk_scale * LOG2_E)).astype(BF16)
    kb_ref[...] = rotary(proj(4)).astype(BF16)


def _rope_inv_lanes():
    half = ROT_DIM // 2
    inv = ROPE_THETA ** (-np.arange(0, ROT_DIM, 2, dtype=np.float32) / ROT_DIM)
    per_head = np.zeros((HEAD_DIM,), np.float32)
    per_head[:half] = inv
    per_head[half:ROT_DIM] = inv
    return jnp.asarray(np.tile(per_head, LANES // HEAD_DIM).reshape(1, LANES))


def _attn_projection(x, mod, norm_g, positions, w_in):
    batch, seq, _ = x.shape
    gw = GROUP_WIDTH
    cols = lambda j: w_in[:, j * gw:(j + 1) * gw]
    w_all = jnp.concatenate(
        [cols(0), cols(1), cols(2), cols(3), cols(4), cols(6), cols(7)], axis=1).astype(BF16)
    w_vbt = cols(5).T.astype(BF16)
    n_cols = w_all.shape[1]
    out_spec = pl.BlockSpec((None, PROJ_ROWS, gw), lambda b, i: (b, i, 0))
    out_shape = jax.ShapeDtypeStruct((batch, seq, gw), BF16)
    out_specs = [out_spec] * 8
    out_shapes = [out_shape] * 8
    out_specs[5] = pl.BlockSpec((None, gw, PROJ_ROWS), lambda b, i: (b, 0, i))
    out_shapes[5] = jax.ShapeDtypeStruct((batch, gw, seq), BF16)
    return pl.pallas_call(
        _proj_kernel,
        grid=(batch, seq // PROJ_ROWS),
        in_specs=[
            pl.BlockSpec((None, PROJ_ROWS, D_MODEL), lambda b, i: (b, i, 0)),
            pl.BlockSpec((None, None, 1, 3 * D_MODEL), lambda b, i: (0, b, 0, 0)),
            pl.BlockSpec((None, 1, D_MODEL), lambda b, i: (0, 0, 0)),
            pl.BlockSpec((None, PROJ_ROWS, 1), lambda b, i: (b, i, 0)),
            pl.BlockSpec((1, LANES), lambda b, i: (0, 0)),
            _resident((D_MODEL, n_cols), lambda b, i: (0, 0)),
            _resident((gw, D_MODEL), lambda b, i: (0, 0)),
        ],
        out_specs=out_specs,
        out_shape=out_shapes,
        compiler_params=pltpu.CompilerParams(
            dimension_semantics=("parallel", "parallel"), vmem_limit_bytes=VMEM_LIMIT),
        name="attn_in_projection",
    )(x, mod.reshape(DEPTH, batch, 1, 3 * D_MODEL), norm_g.reshape(DEPTH, 1, D_MODEL),
      positions.reshape(batch, seq, 1), _rope_inv_lanes(), w_all, w_vbt)


def _head_lane_masks():
    lane = lax.broadcasted_iota(jnp.int32, (1, LANES), 1)
    return lane < HEAD_DIM, lane >= HEAD_DIM


def _split_heads_rows(x, lo_mask, hi_mask):
    zero = jnp.zeros_like(x)
    return jnp.concatenate([jnp.where(lo_mask, x, zero), jnp.where(hi_mask, x, zero)], axis=0)


def _sb_kernel(q_ref, k_ref, v_ref, o_ref, z_ref, logb_ref, later_ref, total_ref, run_ref, acc_ref):
    qi = pl.program_id(1)
    blk = SB_BLOCK
    lo_mask, hi_mask = _head_lane_masks()
    row = lax.broadcasted_iota(jnp.int32, (blk, 2 * blk), 0)
    col = lax.broadcasted_iota(jnp.int32, (blk, 2 * blk), 1)
    below_diag = jnp.where(col < blk, col, col - blk) < row
    r = lax.broadcasted_iota(jnp.int32, (blk, blk), 0)
    c = lax.broadcasted_iota(jnp.int32, (blk, blk), 1)
    later_and_total = jnp.concatenate([(r > c).astype(BF16), jnp.ones((blk, blk), BF16)], axis=1)
    later_and_total = jnp.concatenate([later_and_total, later_and_total], axis=0)
    q_pairs = [q_ref[:, p * LANES:(p + 1) * LANES] for p in range(HEAD_PAIRS)]

    def scores(kb):
        start = pl.multiple_of(jnp.maximum(kb, 0) * blk, blk)
        for p in range(HEAD_PAIRS):
            k2 = _split_heads_rows(k_ref[pl.ds(start, blk), p * LANES:(p + 1) * LANES], lo_mask, hi_mask)
            z_ref[p] = lax.dot_general(q_pairs[p], k2, _NT, preferred_element_type=F32)

    def sweep(kb, diagonal):
        start = pl.multiple_of(kb * blk, blk)
        for p in range(HEAD_PAIRS):
            z = z_ref[p]
            soft = jnp.maximum(z, 0.0) + jnp.log(1.0 + jnp.exp2(jnp.abs(z) * -LOG2_E))
            logb_ref[p] = z - soft
            if diagonal:
                soft = jnp.where(below_diag, soft, 0.0)
            hi = soft.astype(BF16)
            lo = (soft - hi.astype(F32)).astype(BF16)
            for h in range(2):
                split = jnp.concatenate([hi[:, h * blk:(h + 1) * blk], lo[:, h * blk:(h + 1) * blk]], axis=1)
                sums = jnp.dot(split, later_and_total, preferred_element_type=F32)
                later_ref[p, :, h * blk:(h + 1) * blk] = sums[:, :blk]
                total_ref[p, :, h * blk:(h + 1) * blk] = sums[:, blk:]
        scores(kb - 1)
        least = None
        for p in range(HEAD_PAIRS):
            run = run_ref[p]
            w = jnp.exp(logb_ref[p] - later_ref[p] - run)
            if diagonal:
                w = jnp.where(below_diag, w, 0.0)
            v2 = _split_heads_rows(v_ref[pl.ds(start, blk), p * LANES:(p + 1) * LANES], lo_mask, hi_mask)
            acc_ref[p] += jnp.dot(w.astype(BF16), v2, preferred_element_type=F32)
            run = run + total_ref[p]
            run_ref[p] = run
            least = run if least is None else jnp.minimum(least, run)
        return jnp.min(least)

    acc_ref[...] = jnp.zeros_like(acc_ref)
    run_ref[...] = jnp.zeros_like(run_ref)
    scores(qi)
    least = sweep(qi, diagonal=True)

    def more(carry):
        j, least = carry
        return jnp.logical_and(j < qi, least < -F32_EXP_UNDERFLOW)

    def past_block(carry):
        j, _ = carry
        return j + 1, sweep(qi - 1 - j, diagonal=False)

    lax.while_loop(more, past_block, (jnp.int32(0), least))
    for p in range(HEAD_PAIRS):
        o_ref[:, p * LANES:(p + 1) * LANES] = acc_ref[p].astype(o_ref.dtype)


def _sb_attention(q, k, v):
    batch, seq, width = q.shape
    kv_spec = pl.BlockSpec((None, seq, width), lambda b, i: (b, 0, 0))
    q_spec = pl.BlockSpec((None, SB_BLOCK, width), lambda b, i: (b, i, 0))
    return pl.pallas_call(
        _sb_kernel,
        grid=(batch, seq // SB_BLOCK),
        in_specs=[q_spec, kv_spec, kv_spec],
        out_specs=q_spec,
        out_shape=jax.ShapeDtypeStruct(q.shape, BF16),
        scratch_shapes=[pltpu.VMEM((HEAD_PAIRS, SB_BLOCK, 2 * SB_BLOCK), F32)] * 5
        + [pltpu.VMEM((HEAD_PAIRS, SB_BLOCK, LANES), F32)],
        compiler_params=pltpu.CompilerParams(
            dimension_semantics=("parallel", "arbitrary"), vmem_limit_bytes=VMEM_LIMIT),
        name="stick_breaking_attention",
    )(q, k, v)


def _moba_kernel(q_ref, k_ref, vt_ref, o_ref, kmean_ref, sel_ref, s_ref, *, n_blocks):
    qi = pl.program_id(1)
    blk = MOBA_BLOCK
    ones_rows = 16

    @pl.when(qi == 0)
    def _():
        for n in range(n_blocks):
            kn = k_ref[n * blk:(n + 1) * blk, :].astype(F32)
            kmean_ref[n:n + 1, :] = jnp.mean(kn, axis=0, keepdims=True)

    lo_mask, hi_mask = _head_lane_masks()
    blk_row = lax.broadcasted_iota(jnp.int32, (n_blocks, blk), 0)
    blk_row_f = blk_row.astype(F32)
    past = blk_row < qi
    key_row = lax.broadcasted_iota(jnp.int32, (blk, blk), 0)
    query_col = lax.broadcasted_iota(jnp.int32, (blk, blk), 1)
    causal = key_row <= query_col

    q_heads = []
    for p in range(HEAD_PAIRS):
        q_pair = q_ref[:, p * LANES:(p + 1) * LANES]
        kmean = kmean_ref[:, p * LANES:(p + 1) * LANES].astype(BF16)
        for mask in (lo_mask, hi_mask):
            qh = jnp.where(mask, q_pair, jnp.zeros_like(q_pair))
            gate = lax.dot_general(kmean, qh, _NT, preferred_element_type=F32)
            gate = jnp.where(past, gate, -jnp.inf)
            selected = jnp.zeros((n_blocks, blk), F32)
            for _ in range(MOBA_TOPK):
                best = jnp.max(gate, axis=0, keepdims=True)
                first = jnp.min(jnp.where(gate == best, blk_row_f, float(n_blocks)), axis=0, keepdims=True)
                hit = blk_row_f == first
                selected = jnp.where(jnp.logical_and(hit, past), 1.0, selected)
                gate = jnp.where(hit, -jnp.inf, gate)
            sel_ref[len(q_heads)] = selected
            q_heads.append(qh)

    def head_block(h, kb, slot, state, vt_pair, diagonal):
        m, acc = state
        s = s_ref[slot, h]
        if diagonal:
            s = jnp.where(causal, s, -jnp.inf)
            m_new = jnp.maximum(m, jnp.max(s, axis=0, keepdims=True))
            shift = m_new
        else:
            picked = jnp.logical_and(sel_ref[h, pl.ds(jnp.maximum(kb, 0), 1), :] > 0.5, kb >= 0)
            m_new = jnp.where(picked, jnp.maximum(m, jnp.max(s, axis=0, keepdims=True)), m)
            shift = jnp.where(picked, m_new, jnp.inf)
        p = jnp.exp2(s - shift).astype(BF16)
        alpha = jnp.exp2(m - m_new)
        sub = (h % 2) * HEAD_DIM
        vt_aug = jnp.concatenate([vt_pair[sub:sub + HEAD_DIM, :], jnp.ones((ones_rows, blk), BF16)], axis=0)
        acc = alpha * acc + jnp.dot(vt_aug, p, preferred_element_type=F32)
        return m_new, acc

    def score_stage(kb, slot):
        start = pl.multiple_of(jnp.maximum(kb, 0) * blk, blk)
        for p in range(HEAD_PAIRS):
            k_pair = k_ref[pl.ds(start, blk), p * LANES:(p + 1) * LANES]
            for h in (2 * p, 2 * p + 1):
                s_ref[slot, h] = lax.dot_general(k_pair, q_heads[h], _NT, preferred_element_type=F32)

    def value_stage(kb, slot, states, diagonal):
        start = pl.multiple_of(jnp.maximum(kb, 0) * blk, blk)
        new = []
        for p in range(HEAD_PAIRS):
            vt_pair = vt_ref[p * LANES:(p + 1) * LANES, pl.ds(start, blk)]
            for h in (2 * p, 2 * p + 1):
                new.append(head_block(h, kb, slot, states[h], vt_pair, diagonal))
        return tuple(new)

    init = (jnp.full((1, blk), -jnp.inf, F32), jnp.zeros((HEAD_DIM + ones_rows, blk), F32))
    score_stage(qi, 0)
    score_stage(qi - 1, 1)
    states = value_stage(qi, 0, (init,) * N_HEADS, diagonal=True)

    def two_blocks(i, states):
        kb = qi - 1 - 2 * i
        score_stage(kb - 1, 0)
        states = value_stage(kb, 1, states, diagonal=False)
        score_stage(kb - 2, 1)
        return value_stage(kb - 1, 0, states, diagonal=False)

    states = lax.fori_loop(0, (qi + 1) // 2, two_blocks, states)
    for p in range(HEAD_PAIRS):
        outs = []
        for h in (2 * p, 2 * p + 1):
            acc = states[h][1]
            outs.append(acc[0:HEAD_DIM, :] / acc[HEAD_DIM:HEAD_DIM + 1, :])
        o_ref[:, p * LANES:(p + 1) * LANES] = jnp.concatenate(outs, axis=0).T.astype(o_ref.dtype)


def _moba_attention(q, k, vt):
    batch, seq, width = q.shape
    n_blocks = seq // MOBA_BLOCK
    q_spec = pl.BlockSpec((None, MOBA_BLOCK, width), lambda b, i: (b, i, 0))
    return pl.pallas_call(
        functools.partial(_moba_kernel, n_blocks=n_blocks),
        grid=(batch, n_blocks),
        in_specs=[q_spec,
                  pl.BlockSpec((None, seq, width), lambda b, i: (b, 0, 0)),
                  pl.BlockSpec((None, width, seq), lambda b, i: (b, 0, 0))],
        out_specs=q_spec,
        out_shape=jax.ShapeDtypeStruct(q.shape, BF16),
        scratch_shapes=[pltpu.VMEM((n_blocks, width), F32),
                        pltpu.VMEM((N_HEADS, n_blocks, MOBA_BLOCK), F32),
                        pltpu.VMEM((2, N_HEADS, MOBA_BLOCK, MOBA_BLOCK), F32)],
        compiler_params=pltpu.CompilerParams(
            dimension_semantics=("parallel", "arbitrary"), vmem_limit_bytes=VMEM_LIMIT),
        name="moba_attention",
    )(q, k, vt)


def _attn_out_kernel(x_ref, mod_ref, oa_ref, ob_ref, ga_ref, gb_ref, w_ref, o_ref):
    gate = mod_ref[:, 2 * D_MODEL:3 * D_MODEL]
    ya = (oa_ref[...].astype(F32) * ga_ref[...].astype(F32)).astype(BF16)
    yb = (ob_ref[...].astype(F32) * gb_ref[...].astype(F32)).astype(BF16)
    y = (jnp.dot(ya, w_ref[0:GROUP_WIDTH, :], preferred_element_type=F32)
         + jnp.dot(yb, w_ref[GROUP_WIDTH:2 * GROUP_WIDTH, :], preferred_element_type=F32))
    o_ref[...] = x_ref[...] + gate * y


def _attn_output(x, mod, o_a, o_b, g_a, g_b, w_out):
    batch, seq, _ = x.shape
    row_spec = pl.BlockSpec((None, PROJ_ROWS, D_MODEL), lambda b, i: (b, i, 0))
    half_spec = pl.BlockSpec((None, PROJ_ROWS, GROUP_WIDTH), lambda b, i: (b, i, 0))
    return pl.pallas_call(
        _attn_out_kernel,
        grid=(batch, seq // PROJ_ROWS),
        in_specs=[
            row_spec,
            pl.BlockSpec((None, None, 1, 3 * D_MODEL), lambda b, i: (0, b, 0, 0)),
            half_spec, half_spec, half_spec, half_spec,
            _resident((D_MODEL, D_MODEL), lambda b, i: (0, 0)),
        ],
        out_specs=row_spec,
        out_shape=jax.ShapeDtypeStruct(x.shape, F32),
        compiler_params=pltpu.CompilerParams(
            dimension_semantics=("parallel", "parallel"), vmem_limit_bytes=VMEM_LIMIT),
        name="attn_out_projection",
    )(x, mod.reshape(DEPTH, batch, 1, 3 * D_MODEL), o_a, o_b, g_a, g_b, w_out.astype(BF16))


def _lru_kernel(x_ref, mod_ref, g_ref, w_in_ref, conv_w_ref, conv_b_ref, w_gate_ref,
                b_a_ref, b_x_ref, lam_ref, w_out_ref, final_g_ref, o_ref,
                h_ref, xb_ref, xc_ref, sg_ref, a_ref, b_ref, y_ref, state_ref):
    step_i = pl.program_id(1)
    nb, nt, pitch = LRU_BATCH, LRU_STEPS, LRU_PITCH
    width = LRU_BLOCK_WIDTH

    @pl.when(step_i == 0)
    def _():
        state_ref[...] = jnp.zeros_like(state_ref)
        xb_ref[:, 0:8, :] = jnp.zeros((nb, 8, D_MODEL), F32)

    for b in range(nb):
        shift = mod_ref[b, :, 0:D_MODEL]
        scale = mod_ref[b, :, D_MODEL:2 * D_MODEL]
        h = _rms(x_ref[b], g_ref[...]) * (1.0 + scale) + shift
        h_ref[b * nt:(b + 1) * nt, :] = h.astype(BF16)

    xb = jnp.dot(h_ref[...], w_in_ref[:, 0:D_MODEL], preferred_element_type=F32)
    for b in range(nb):
        xb_ref[b, 8:8 + nt, :] = xb[b * nt:(b + 1) * nt, :]
    gate_branch = jnp.dot(h_ref[...], w_in_ref[:, D_MODEL:2 * D_MODEL], preferred_element_type=F32)
    sg_ref[...] = jax.nn.silu(gate_branch)

    for b in range(nb):
        xc = conv_b_ref[...] + conv_w_ref[CONV_WIDTH - 1:CONV_WIDTH, :] * xb_ref[b, 8:8 + nt, :]
        for tap in range(CONV_WIDTH - 1):
            off = 8 - (CONV_WIDTH - 1) + tap
            xc = xc + conv_w_ref[tap:tap + 1, :] * xb_ref[b, off:off + nt, :]
        xc_ref[b * nt:(b + 1) * nt, :] = xc
        xb_ref[b, 0:8, :] = xb_ref[b, nt:nt + 8, :]

    for n in range(LRU_BLOCKS):
        cols = slice(n * width, (n + 1) * width)
        xc_n = xc_ref[:, cols]
        gates = jnp.dot(xc_n.astype(BF16), w_gate_ref[n], preferred_element_type=F32)
        r = jax.nn.sigmoid(gates[:, 0:width] + b_a_ref[:, cols])
        inp = jax.nn.sigmoid(gates[:, width:2 * width] + b_x_ref[:, cols])
        log_a = LRU_C * r * jax.nn.log_sigmoid(lam_ref[:, cols])
        a = jnp.exp(log_a)
        gain_sq = -jnp.tanh(log_a) * (a * a + 1.0)
        mult = jnp.where(gain_sq > 0.0, gain_sq * lax.rsqrt(gain_sq), 0.0)
        b_term = mult * (inp * xc_n)
        for b in range(nb):
            a_ref[n, b * pitch:b * pitch + nt, :] = a[b * nt:(b + 1) * nt, :]
            b_ref[n, b * pitch:b * pitch + nt, :] = b_term[b * nt:(b + 1) * nt, :]

    def scan_step(t, hs):
        new = []
        for n in range(LRU_BLOCKS):
            rows = pl.ds(t, nb, stride=pitch)
            h_n = a_ref[n, rows, :] * hs[n] + b_ref[n, rows, :]
            b_ref[n, rows, :] = h_n
            new.append(h_n)
        return tuple(new)

    hs = lax.fori_loop(0, nt, scan_step, tuple(state_ref[n] for n in range(LRU_BLOCKS)))
    for n in range(LRU_BLOCKS):
        state_ref[n] = hs[n]

    for b in range(nb):
        hs_b = jnp.concatenate(
            [b_ref[n, b * pitch:b * pitch + nt, :] for n in range(LRU_BLOCKS)], axis=1)
        y_ref[b * nt:(b + 1) * nt, :] = (hs_b * sg_ref[b * nt:(b + 1) * nt, :]).astype(BF16)

    y = jnp.dot(y_ref[...], w_out_ref[...], preferred_element_type=F32)
    for b in range(nb):
        gate = mod_ref[b, :, 2 * D_MODEL:3 * D_MODEL]
        x_new = x_ref[b] + gate * y[b * nt:(b + 1) * nt, :]
        o_ref[b] = _rms(x_new, final_g_ref[...])


def _lru_layer(x, mod, norm_g, w_in, conv_w, conv_b, w_a, b_a, w_x, b_x, lam, w_out, final_g):
    batch, seq, _ = x.shape
    rows = LRU_BATCH * LRU_STEPS
    x_spec = pl.BlockSpec((LRU_BATCH, LRU_STEPS, D_MODEL), lambda g, i: (g, i, 0))
    vec_spec = pl.BlockSpec((1, D_MODEL), lambda g, i: (0, 0))
    w_gate = jnp.concatenate([w_a, w_x], axis=-1).astype(BF16)
    return pl.pallas_call(
        _lru_kernel,
        grid=(batch // LRU_BATCH, seq // LRU_STEPS),
        in_specs=[
            x_spec,
            pl.BlockSpec((None, LRU_BATCH, 1, 3 * D_MODEL), lambda g, i: (1, g, 0, 0)),
            pl.BlockSpec((None, 1, D_MODEL), lambda g, i: (1, 0, 0)),
            _resident((D_MODEL, 2 * D_MODEL), lambda g, i: (0, 0)),
            pl.BlockSpec((CONV_WIDTH, D_MODEL), lambda g, i: (0, 0)),
            vec_spec,
            _resident((LRU_BLOCKS, LRU_BLOCK_WIDTH, 2 * LRU_BLOCK_WIDTH), lambda g, i: (0, 0, 0)),
            vec_spec, vec_spec, vec_spec,
            _resident((D_MODEL, D_MODEL), lambda g, i: (0, 0)),
            vec_spec,
        ],
        out_specs=x_spec,
        out_shape=jax.ShapeDtypeStruct(x.shape, F32),
        scratch_shapes=[
            pltpu.VMEM((rows, D_MODEL), BF16),
            pltpu.VMEM((LRU_BATCH, 8 + LRU_STEPS, D_MODEL), F32),
            pltpu.VMEM((rows, D_MODEL), F32),
            pltpu.VMEM((rows, D_MODEL), F32),
            pltpu.VMEM((LRU_BLOCKS, LRU_BATCH * LRU_PITCH, LRU_BLOCK_WIDTH), F32),
            pltpu.VMEM((LRU_BLOCKS, LRU_BATCH * LRU_PITCH, LRU_BLOCK_WIDTH), F32),
            pltpu.VMEM((rows, D_MODEL), BF16),
            pltpu.VMEM((LRU_BLOCKS, LRU_BATCH, LRU_BLOCK_WIDTH), F32),
        ],
        compiler_params=pltpu.CompilerParams(
            dimension_semantics=("parallel", "arbitrary"), vmem_limit_bytes=VMEM_LIMIT),
        name="rglru_layer_final_norm",
    )(x, mod.reshape(DEPTH, batch, 1, 3 * D_MODEL), norm_g.reshape(DEPTH, 1, D_MODEL),
      w_in.astype(BF16), conv_w, conv_b.reshape(1, D_MODEL), w_gate,
      b_a.reshape(1, D_MODEL), b_x.reshape(1, D_MODEL), lam.reshape(1, D_MODEL),
      w_out.astype(BF16), final_g.reshape(1, D_MODEL))


def kernel(x, c, positions, norm_g, w_mod, b_mod, attn_w_in, attn_w_out, lru_w_in, lru_conv_w,
           lru_conv_b, lru_w_a, lru_b_a, lru_w_x, lru_b_x, lru_lambda, lru_w_out, final_g):
    batch, seq, d_model = x.shape
    assert d_model == D_MODEL and DEPTH == 2
    assert batch % LRU_BATCH == 0 and seq % PROJ_ROWS == 0 and seq % MOBA_BLOCK == 0
    mod = _modulation(c, w_mod, b_mod)
    q_a, k_a, v_a, q_b, k_b, vt_b, g_a, g_b = _attn_projection(x, mod, norm_g, positions, attn_w_in[0])
    o_a = _sb_attention(q_a, k_a, v_a)
    o_b = _moba_attention(q_b, k_b, vt_b)
    x = _attn_output(x, mod, o_a, o_b, g_a, g_b, attn_w_out[0])
    return _lru_layer(x, mod, norm_g, lru_w_in[0], lru_conv_w[0], lru_conv_b[0], lru_w_a[0],
                      lru_b_a[0], lru_w_x[0], lru_b_x[0], lru_lambda[0], lru_w_out[0], final_g)
```

```python
import functools

import jax
import jax.numpy as jnp
import numpy as np
from jax import lax
from jax.experimental import pallas as pl
from jax.experimental.pallas import tpu as pltpu

F32 = jnp.float32
BF16 = jnp.bfloat16

D_MODEL = 1024
DEPTH = 2
HEAD_DIM = 64
N_HEADS = 8
GROUP_WIDTH = N_HEADS * HEAD_DIM
LANES = 128
HEAD_PAIRS = GROUP_WIDTH // LANES
SB_BLOCK = 128
MOBA_BLOCK = 256
MOBA_TOPK = 3
ROPE_THETA = 500000.0
ROT_DIM = HEAD_DIM // 4
LRU_BLOCKS = 8
LRU_BLOCK_WIDTH = D_MODEL // LRU_BLOCKS
CONV_WIDTH = 4
LRU_C = 8.0
EPS = 1e-6

PROJ_ROWS = 512
LRU_BATCH = 8
LRU_STEPS = 64
LRU_PITCH = LRU_STEPS + 8
VMEM_LIMIT = 56 * 1024 * 1024
F32_EXP_UNDERFLOW = -104.0
LOG2_E = 1.4426950408889634

_NT = (((1,), (1,)), ((), ()))


def _resident(shape, index_map):
    return pl.BlockSpec(shape, index_map, pipeline_mode=pl.Buffered(1))


def _rms(x, g):
    ms = jnp.mean(x * x, axis=-1, keepdims=True)
    return x * lax.rsqrt(ms + EPS) * g


def _mod_kernel(c_ref, w_ref, b_ref, o_ref):
    o_ref[...] = jnp.dot(c_ref[...], w_ref[...], preferred_element_type=F32,
                         precision=lax.Precision.HIGHEST) + b_ref[...]


def _modulation(c, w_mod, b_mod):
    batch = c.shape[0]
    return pl.pallas_call(
        _mod_kernel,
        grid=(DEPTH, 3),
        in_specs=[
            pl.BlockSpec((batch, D_MODEL), lambda l, j: (0, 0)),
            pl.BlockSpec((None, D_MODEL, D_MODEL), lambda l, j: (l, 0, j)),
            pl.BlockSpec((None, 1, D_MODEL), lambda l, j: (l, 0, j)),
        ],
        out_specs=pl.BlockSpec((None, batch, D_MODEL), lambda l, j: (l, 0, j)),
        out_shape=jax.ShapeDtypeStruct((DEPTH, batch, 3 * D_MODEL), F32),
        name="adaln_modulation",
    )(c, w_mod, b_mod.reshape(DEPTH, 1, 3 * D_MODEL))


def _proj_kernel(x_ref, mod_ref, g_ref, pos_ref, inv_ref, w_ref, wvt_ref,
                 qa_ref, ka_ref, va_ref, qb_ref, kb_ref, vbt_ref, ga_ref, gb_ref):
    shift = mod_ref[:, 0:D_MODEL]
    scale = mod_ref[:, D_MODEL:2 * D_MODEL]
    qk_scale = HEAD_DIM ** -0.5
    half = ROT_DIM // 2
    in_head = lax.broadcasted_iota(jnp.int32, (1, LANES), 1) % HEAD_DIM
    from_below = jnp.logical_and(in_head >= half, in_head < ROT_DIM)
    from_above = in_head < half

    for rows in (slice(0, PROJ_ROWS // 2), slice(PROJ_ROWS // 2, PROJ_ROWS)):
        h = (_rms(x_ref[rows, :], g_ref[...]) * (1.0 + scale) + shift).astype(BF16)

        def proj(j):
            return jnp.dot(h, w_ref[:, j * GROUP_WIDTH:(j + 1) * GROUP_WIDTH],
                           preferred_element_type=F32)

        ang = pos_ref[rows, :].astype(F32) * inv_ref[...]
        cos, sin = jnp.cos(ang), jnp.sin(ang)
        sin_from_below = jnp.where(from_below, sin, 0.0)
        sin_from_above = jnp.where(from_above, -sin, 0.0)

        def rotary(t):
            out = []
            for p in range(HEAD_PAIRS):
                tp = t[:, p * LANES:(p + 1) * LANES]
                out.append(tp * cos + pltpu.roll(tp, half, 1) * sin_from_below
                           + pltpu.roll(tp, LANES - half, 1) * sin_from_above)
            return jnp.concatenate(out, axis=1)

        qb_ref[rows, :] = (rotary(proj(3)) * (qk_scale * LOG2_E)).astype(BF16)
        kb_ref[rows, :] = rotary(proj(4)).astype(BF16)
        qa_ref[rows, :] = (proj(0) * qk_scale).astype(BF16)
        ka_ref[rows, :] = proj(1).astype(BF16)
        va_ref[rows, :] = proj(2).astype(BF16)
        vbt_ref[:, rows] = lax.dot_general(wvt_ref[...], h, _NT, preferred_element_type=F32).astype(BF16)
        ga_ref[rows, :] = jax.nn.silu(proj(5)).astype(BF16)
        gb_ref[rows, :] = jax.nn.silu(proj(6)).astype(BF16)


def _rope_inv_lanes():
    half = ROT_DIM // 2
    inv = ROPE_THETA ** (-np.arange(0, ROT_DIM, 2, dtype=np.float32) / ROT_DIM)
    per_head = np.zeros((HEAD_DIM,), np.float32)
    per_head[:half] = inv
    per_head[half:ROT_DIM] = inv
    return jnp.asarray(np.tile(per_head, LANES // HEAD_DIM).reshape(1, LANES))


def _attn_projection(x, mod, norm_g, positions, w_in):
    batch, seq, _ = x.shape
    gw = GROUP_WIDTH
    cols = lambda j: w_in[:, j * gw:(j + 1) * gw]
    w_all = jnp.concatenate(
        [cols(0), cols(1), cols(2), cols(3), cols(4), cols(6), cols(7)], axis=1).astype(BF16)
    w_vbt = cols(5).T.astype(BF16)
    n_cols = w_all.shape[1]
    out_spec = pl.BlockSpec((None, PROJ_ROWS, gw), lambda b, i: (b, i, 0))
    out_shape = jax.ShapeDtypeStruct((batch, seq, gw), BF16)
    out_specs = [out_spec] * 8
    out_shapes = [out_shape] * 8
    out_specs[5] = pl.BlockSpec((None, gw, PROJ_ROWS), lambda b, i: (b, 0, i))
    out_shapes[5] = jax.ShapeDtypeStruct((batch, gw, seq), BF16)
    return pl.pallas_call(
        _proj_kernel,
        grid=(batch, seq // PROJ_ROWS),
        in_specs=[
            pl.BlockSpec((None, PROJ_ROWS, D_MODEL), lambda b, i: (b, i, 0)),
            pl.BlockSpec((None, None, 1, 3 * D_MODEL), lambda b, i: (0, b, 0, 0)),
            pl.BlockSpec((None, 1, D_MODEL), lambda b, i: (0, 0, 0)),
            pl.BlockSpec((None, PROJ_ROWS, 1), lambda b, i: (b, i, 0)),
            pl.BlockSpec((1, LANES), lambda b, i: (0, 0)),
            _resident((D_MODEL, n_cols), lambda b, i: (0, 0)),
            _resident((gw, D_MODEL), lambda b, i: (0, 0)),
        ],
        out_specs=out_specs,
        out_shape=out_shapes,
        compiler_params=pltpu.CompilerParams(
            dimension_semantics=("parallel", "parallel"), vmem_limit_bytes=VMEM_LIMIT),
        name="attn_in_projection",
    )(x, mod.reshape(DEPTH, batch, 1, 3 * D_MODEL), norm_g.reshape(DEPTH, 1, D_MODEL),
      positions.reshape(batch, seq, 1), _rope_inv_lanes(), w_all, w_vbt)


def _head_lane_masks():
    lane = lax.broadcasted_iota(jnp.int32, (1, LANES), 1)
    return lane < HEAD_DIM, lane >= HEAD_DIM


def _split_heads_rows(x, lo_mask, hi_mask):
    zero = jnp.zeros_like(x)
    return jnp.concatenate([jnp.where(lo_mask, x, zero), jnp.where(hi_mask, x, zero)], axis=0)


def _sb_kernel(q_ref, k_ref, v_ref, o_ref, z_ref, logb_ref, later_ref, total_ref, run_ref, acc_ref):
    qi = pl.program_id(1)
    blk = SB_BLOCK
    lo_mask, hi_mask = _head_lane_masks()
    row = lax.broadcasted_iota(jnp.int32, (blk, 2 * blk), 0)
    col = lax.broadcasted_iota(jnp.int32, (blk, 2 * blk), 1)
    below_diag = jnp.where(col < blk, col, col - blk) < row
    r = lax.broadcasted_iota(jnp.int32, (blk, blk), 0)
    c = lax.broadcasted_iota(jnp.int32, (blk, blk), 1)
    later_and_total = jnp.concatenate([(r > c).astype(BF16), jnp.ones((blk, blk), BF16)], axis=1)
    later_and_total = jnp.concatenate([later_and_total, later_and_total], axis=0)
    q_pairs = [q_ref[:, p * LANES:(p + 1) * LANES] for p in range(HEAD_PAIRS)]

    def scores(kb):
        start = pl.multiple_of(jnp.maximum(kb, 0) * blk, blk)
        for p in range(HEAD_PAIRS):
            k2 = _split_heads_rows(k_ref[pl.ds(start, blk), p * LANES:(p + 1) * LANES], lo_mask, hi_mask)
            z_ref[p] = lax.dot_general(q_pairs[p], k2, _NT, preferred_element_type=F32)

    def sweep(kb, diagonal):
        start = pl.multiple_of(kb * blk, blk)
        for p in range(HEAD_PAIRS):
            z = z_ref[p]
            soft = jnp.maximum(z, 0.0) + jnp.log(1.0 + jnp.exp2(jnp.abs(z) * -LOG2_E))
            logb_ref[p] = z - soft
            if diagonal:
                soft = jnp.where(below_diag, soft, 0.0)
            hi = soft.astype(BF16)
            lo = (soft - hi.astype(F32)).astype(BF16)
            for h in range(2):
                split = jnp.concatenate([hi[:, h * blk:(h + 1) * blk], lo[:, h * blk:(h + 1) * blk]], axis=1)
                sums = jnp.dot(split, later_and_total, preferred_element_type=F32)
                later_ref[p, :, h * blk:(h + 1) * blk] = sums[:, :blk]
                total_ref[p, :, h * blk:(h + 1) * blk] = sums[:, blk:]
        scores(kb - 1)
        least = None
        for p in range(HEAD_PAIRS):
            run = run_ref[p]
            w = jnp.exp(logb_ref[p] - later_ref[p] - run)
            if diagonal:
                w = jnp.where(below_diag, w, 0.0)
            v2 = _split_heads_rows(v_ref[pl.ds(start, blk), p * LANES:(p + 1) * LANES], lo_mask, hi_mask)
            acc_ref[p] += jnp.dot(w.astype(BF16), v2, preferred_element_type=F32)
            run = run + total_ref[p]
            run_ref[p] = run
            least = run if least is None else jnp.minimum(least, run)
        return jnp.min(least)

    acc_ref[...] = jnp.zeros_like(acc_ref)
    run_ref[...] = jnp.zeros_like(run_ref)
    scores(qi)
    least = sweep(qi, diagonal=True)

    def more(carry):
        j, least = carry
        return jnp.logical_and(j < qi, least < -F32_EXP_UNDERFLOW)

    def past_block(carry):
        j, _ = carry
        return j + 1, sweep(qi - 1 - j, diagonal=False)

    lax.while_loop(more, past_block, (jnp.int32(0), least))
    for p in range(HEAD_PAIRS):
        o_ref[:, p * LANES:(p + 1) * LANES] = acc_ref[p].astype(o_ref.dtype)


def _sb_attention(q, k, v):
    batch, seq, width = q.shape
    kv_spec = pl.BlockSpec((None, seq, width), lambda b, i: (b, 0, 0))
    q_spec = pl.BlockSpec((None, SB_BLOCK, width), lambda b, i: (b, i, 0))
    return pl.pallas_call(
        _sb_kernel,
        grid=(batch, seq // SB_BLOCK),
        in_specs=[q_spec, kv_spec, kv_spec],
        out_specs=q_spec,
        out_shape=jax.ShapeDtypeStruct(q.shape, BF16),
        scratch_shapes=[pltpu.VMEM((HEAD_PAIRS, SB_BLOCK, 2 * SB_BLOCK), F32)] * 5
        + [pltpu.VMEM((HEAD_PAIRS, SB_BLOCK, LANES), F32)],
        compiler_params=pltpu.CompilerParams(
            dimension_semantics=("parallel", "arbitrary"), vmem_limit_bytes=VMEM_LIMIT),
        name="stick_breaking_attention",
    )(q, k, v)


def _moba_kernel(q_ref, k_ref, vt_ref, o_ref, kmean_ref, sel_ref, s_ref, *, n_blocks):
    qi = pl.program_id(1)
    blk = MOBA_BLOCK
    ones_rows = 16

    @pl.when(qi == 0)
    def _():
        for n in range(n_blocks):
            kn = k_ref[n * blk:(n + 1) * blk, :].astype(F32)
            kmean_ref[n:n + 1, :] = jnp.mean(kn, axis=0, keepdims=True)

    lo_mask, hi_mask = _head_lane_masks()
    blk_row = lax.broadcasted_iota(jnp.int32, (n_blocks, blk), 0)
    blk_row_f = blk_row.astype(F32)
    past = blk_row < qi
    key_row = lax.broadcasted_iota(jnp.int32, (blk, blk), 0)
    query_col = lax.broadcasted_iota(jnp.int32, (blk, blk), 1)
    causal = key_row <= query_col

    q_heads = []
    for p in range(HEAD_PAIRS):
        q_pair = q_ref[:, p * LANES:(p + 1) * LANES]
        kmean = kmean_ref[:, p * LANES:(p + 1) * LANES].astype(BF16)
        for mask in (lo_mask, hi_mask):
            qh = jnp.where(mask, q_pair, jnp.zeros_like(q_pair))
            gate = lax.dot_general(kmean, qh, _NT, preferred_element_type=F32)
            gate = jnp.where(past, gate, -jnp.inf)
            selected = jnp.zeros((n_blocks, blk), F32)
            for _ in range(MOBA_TOPK):
                best = jnp.max(gate, axis=0, keepdims=True)
                first = jnp.min(jnp.where(gate == best, blk_row_f, float(n_blocks)), axis=0, keepdims=True)
                hit = blk_row_f == first
                selected = jnp.where(jnp.logical_and(hit, past), 1.0, selected)
                gate = jnp.where(hit, -jnp.inf, gate)
            sel_ref[len(q_heads)] = selected
            q_heads.append(qh)

    def head_block(h, kb, slot, state, vt_pair, diagonal):
        m, acc = state
        s = s_ref[slot, h]
        if diagonal:
            s = jnp.where(causal, s, -jnp.inf)
            m_new = jnp.maximum(m, jnp.max(s, axis=0, keepdims=True))
            shift = m_new
        else:
            picked = jnp.logical_and(sel_ref[h, pl.ds(jnp.maximum(kb, 0), 1), :] > 0.5, kb >= 0)
            m_new = jnp.where(picked, jnp.maximum(m, jnp.max(s, axis=0, keepdims=True)), m)
            shift = jnp.where(picked, m_new, jnp.inf)
        p = jnp.exp2(s - shift).astype(BF16)
        alpha = jnp.exp2(m - m_new)
        sub = (h % 2) * HEAD_DIM
        vt_aug = jnp.concatenate([vt_pair[sub:sub + HEAD_DIM, :], jnp.ones((ones_rows, blk), BF16)], axis=0)
        acc = alpha * acc + jnp.dot(vt_aug, p, preferred_element_type=F32)
        return m_new, acc

    def score_stage(kb, slot):
        start = pl.multiple_of(jnp.maximum(kb, 0) * blk, blk)
        for p in range(HEAD_PAIRS):
            k_pair = k_ref[pl.ds(start, blk), p * LANES:(p + 1) * LANES]
            for h in (2 * p, 2 * p + 1):
                s_ref[slot, h] = lax.dot_general(k_pair, q_heads[h], _NT, preferred_element_type=F32)

    def value_stage(kb, slot, states, diagonal):
        start = pl.multiple_of(jnp.maximum(kb, 0) * blk, blk)
        new = []
        for p in range(HEAD_PAIRS):
            vt_pair = vt_ref[p * LANES:(p + 1) * LANES, pl.ds(start, blk)]
            for h in (2 * p, 2 * p + 1):
                new.append(head_block(h, kb, slot, states[h], vt_pair, diagonal))
        return tuple(new)

    init = (jnp.full((1, blk), -jnp.inf, F32), jnp.zeros((HEAD_DIM + ones_rows, blk), F32))
    score_stage(qi, 0)
    score_stage(qi - 1, 1)
    states = value_stage(qi, 0, (init,) * N_HEADS, diagonal=True)

    def two_blocks(i, states):
        kb = qi - 1 - 2 * i
        score_stage(kb - 1, 0)
        states = value_stage(kb, 1, states, diagonal=False)
        score_stage(kb - 2, 1)
        return value_stage(kb - 1, 0, states, diagonal=False)

    states = lax.fori_loop(0, (qi + 1) // 2, two_blocks, states)
    for p in range(HEAD_PAIRS):
        outs = []
        for h in (2 * p, 2 * p + 1):
            acc = states[h][1]
            outs.append(acc[0:HEAD_DIM, :] / acc[HEAD_DIM:HEAD_DIM + 1, :])
        o_ref[:, p * LANES:(p + 1) * LANES] = jnp.concatenate(outs, axis=0).T.astype(o_ref.dtype)


def _moba_attention(q, k, vt):
    batch, seq, width = q.shape
    n_blocks = seq // MOBA_BLOCK
    q_spec = pl.BlockSpec((None, MOBA_BLOCK, width), lambda b, i: (b, i, 0))
    return pl.pallas_call(
        functools.partial(_moba_kernel, n_blocks=n_blocks),
        grid=(batch, n_blocks),
        in_specs=[q_spec,
                  pl.BlockSpec((None, seq, width), lambda b, i: (b, 0, 0)),
                  pl.BlockSpec((None, width, seq), lambda b, i: (b, 0, 0))],
        out_specs=q_spec,
        out_shape=jax.ShapeDtypeStruct(q.shape, BF16),
        scratch_shapes=[pltpu.VMEM((n_blocks, width), F32),
                        pltpu.VMEM((N_HEADS, n_blocks, MOBA_BLOCK), F32),
                        pltpu.VMEM((2, N_HEADS, MOBA_BLOCK, MOBA_BLOCK), F32)],
        compiler_params=pltpu.CompilerParams(
            dimension_semantics=("parallel", "arbitrary"), vmem_limit_bytes=VMEM_LIMIT),
        name="moba_attention",
    )(q, k, vt)


def _lru_kernel(x_ref, mod0_ref, oa_ref, ob_ref, ga_ref, gb_ref, w_attn_ref,
                mod_ref, g_ref, w_in_ref, conv_w_ref, conv_b_ref, w_gate_ref,
                b_a_ref, b_x_ref, lam_ref, w_out_ref, final_g_ref, o_ref,
                x1_ref, h_ref, xb_ref, xc_ref, sg_ref, a_ref, b_ref, y_ref, state_ref):
    step_i = pl.program_id(1)
    nb, nt, pitch = LRU_BATCH, LRU_STEPS, LRU_PITCH
    width = LRU_BLOCK_WIDTH

    @pl.when(step_i == 0)
    def _():
        state_ref[...] = jnp.zeros_like(state_ref)
        xb_ref[:, 0:8, :] = jnp.zeros((nb, 8, D_MODEL), F32)

    for b in range(nb):
        y_ref[b * nt:(b + 1) * nt, 0:GROUP_WIDTH] = oa_ref[b] * ga_ref[b]
        y_ref[b * nt:(b + 1) * nt, GROUP_WIDTH:2 * GROUP_WIDTH] = ob_ref[b] * gb_ref[b]
    y0 = jnp.dot(y_ref[...], w_attn_ref[...], preferred_element_type=F32)

    for b in range(nb):
        x1 = x_ref[b] + mod0_ref[b, :, 2 * D_MODEL:3 * D_MODEL] * y0[b * nt:(b + 1) * nt, :]
        x1_ref[b] = x1
        shift = mod_ref[b, :, 0:D_MODEL]
        scale = mod_ref[b, :, D_MODEL:2 * D_MODEL]
        h = _rms(x1, g_ref[...]) * (1.0 + scale) + shift
        h_ref[b * nt:(b + 1) * nt, :] = h.astype(BF16)

    xb = jnp.dot(h_ref[...], w_in_ref[:, 0:D_MODEL], preferred_element_type=F32)
    for b in range(nb):
        xb_ref[b, 8:8 + nt, :] = xb[b * nt:(b + 1) * nt, :]
    gate_branch = jnp.dot(h_ref[...], w_in_ref[:, D_MODEL:2 * D_MODEL], preferred_element_type=F32)
    sg_ref[...] = jax.nn.silu(gate_branch)

    for b in range(nb):
        xc = conv_b_ref[...] + conv_w_ref[CONV_WIDTH - 1:CONV_WIDTH, :] * xb_ref[b, 8:8 + nt, :]
        for tap in range(CONV_WIDTH - 1):
            off = 8 - (CONV_WIDTH - 1) + tap
            xc = xc + conv_w_ref[tap:tap + 1, :] * xb_ref[b, off:off + nt, :]
        xc_ref[b * nt:(b + 1) * nt, :] = xc
        xb_ref[b, 0:8, :] = xb_ref[b, nt:nt + 8, :]

    for n in range(LRU_BLOCKS):
        cols = slice(n * width, (n + 1) * width)
        xc_n = xc_ref[:, cols]
        gates = jnp.dot(xc_n.astype(BF16), w_gate_ref[n], preferred_element_type=F32)
        r = jax.nn.sigmoid(gates[:, 0:width] + b_a_ref[:, cols])
        inp = jax.nn.sigmoid(gates[:, width:2 * width] + b_x_ref[:, cols])
        log_a = LRU_C * r * jax.nn.log_sigmoid(lam_ref[:, cols])
        a = jnp.exp(log_a)
        gain_sq = -jnp.tanh(log_a) * (a * a + 1.0)
        mult = jnp.where(gain_sq > 0.0, gain_sq * lax.rsqrt(gain_sq), 0.0)
        b_term = mult * (inp * xc_n)
        for b in range(nb):
            a_ref[n, b * pitch:b * pitch + nt, :] = a[b * nt:(b + 1) * nt, :]
            b_ref[n, b * pitch:b * pitch + nt, :] = b_term[b * nt:(b + 1) * nt, :]

    def scan_step(t, hs):
        new = []
        for n in range(LRU_BLOCKS):
            rows = pl.ds(t, nb, stride=pitch)
            h_n = a_ref[n, rows, :] * hs[n] + b_ref[n, rows, :]
            b_ref[n, rows, :] = h_n
            new.append(h_n)
        return tuple(new)

    hs = lax.fori_loop(0, nt, scan_step, tuple(state_ref[n] for n in range(LRU_BLOCKS)), unroll=4)
    for n in range(LRU_BLOCKS):
        state_ref[n] = hs[n]

    for b in range(nb):
        hs_b = jnp.concatenate(
            [b_ref[n, b * pitch:b * pitch + nt, :] for n in range(LRU_BLOCKS)], axis=1)
        y_ref[b * nt:(b + 1) * nt, :] = (hs_b * sg_ref[b * nt:(b + 1) * nt, :]).astype(BF16)

    y = jnp.dot(y_ref[...], w_out_ref[...], preferred_element_type=F32)
    for b in range(nb):
        gate = mod_ref[b, :, 2 * D_MODEL:3 * D_MODEL]
        x_new = x1_ref[b] + gate * y[b * nt:(b + 1) * nt, :]
        o_ref[b] = _rms(x_new, final_g_ref[...])


def _lru_layer(x, mod, o_a, o_b, g_a, g_b, w_attn_out, norm_g, w_in, conv_w, conv_b, w_a, b_a, w_x, b_x,
               lam, w_out, final_g):
    batch, seq, _ = x.shape
    rows = LRU_BATCH * LRU_STEPS
    x_spec = pl.BlockSpec((LRU_BATCH, LRU_STEPS, D_MODEL), lambda g, i: (g, i, 0))
    half_spec = pl.BlockSpec((LRU_BATCH, LRU_STEPS, GROUP_WIDTH), lambda g, i: (g, i, 0))
    vec_spec = pl.BlockSpec((1, D_MODEL), lambda g, i: (0, 0))
    mod4 = mod.reshape(DEPTH, batch, 1, 3 * D_MODEL)
    w_gate = jnp.concatenate([w_a, w_x], axis=-1).astype(BF16)
    return pl.pallas_call(
        _lru_kernel,
        grid=(batch // LRU_BATCH, seq // LRU_STEPS),
        in_specs=[
            x_spec,
            pl.BlockSpec((None, LRU_BATCH, 1, 3 * D_MODEL), lambda g, i: (0, g, 0, 0)),
            half_spec, half_spec, half_spec, half_spec,
            _resident((D_MODEL, D_MODEL), lambda g, i: (0, 0)),
            pl.BlockSpec((None, LRU_BATCH, 1, 3 * D_MODEL), lambda g, i: (1, g, 0, 0)),
            pl.BlockSpec((None, 1, D_MODEL), lambda g, i: (1, 0, 0)),
            _resident((D_MODEL, 2 * D_MODEL), lambda g, i: (0, 0)),
            pl.BlockSpec((CONV_WIDTH, D_MODEL), lambda g, i: (0, 0)),
            vec_spec,
            _resident((LRU_BLOCKS, LRU_BLOCK_WIDTH, 2 * LRU_BLOCK_WIDTH), lambda g, i: (0, 0, 0)),
            vec_spec, vec_spec, vec_spec,
            _resident((D_MODEL, D_MODEL), lambda g, i: (0, 0)),
            vec_spec,
        ],
        out_specs=x_spec,
        out_shape=jax.ShapeDtypeStruct(x.shape, F32),
        scratch_shapes=[
            pltpu.VMEM((LRU_BATCH, LRU_STEPS, D_MODEL), F32),
            pltpu.VMEM((rows, D_MODEL), BF16),
            pltpu.VMEM((LRU_BATCH, 8 + LRU_STEPS, D_MODEL), F32),
            pltpu.VMEM((rows, D_MODEL), F32),
            pltpu.VMEM((rows, D_MODEL), F32),
            pltpu.VMEM((LRU_BLOCKS, LRU_BATCH * LRU_PITCH, LRU_BLOCK_WIDTH), F32),
            pltpu.VMEM((LRU_BLOCKS, LRU_BATCH * LRU_PITCH, LRU_BLOCK_WIDTH), F32),
            pltpu.VMEM((rows, D_MODEL), BF16),
            pltpu.VMEM((LRU_BLOCKS, LRU_BATCH, LRU_BLOCK_WIDTH), F32),
        ],
        compiler_params=pltpu.CompilerParams(
            dimension_semantics=("parallel", "arbitrary"), vmem_limit_bytes=VMEM_LIMIT),
        name="rglru_layer_final_norm",
    )(x, mod4, o_a, o_b, g_a, g_b, w_attn_out.astype(BF16), mod4, norm_g.reshape(DEPTH, 1, D_MODEL),
      w_in.astype(BF16), conv_w, conv_b.reshape(1, D_MODEL), w_gate,
      b_a.reshape(1, D_MODEL), b_x.reshape(1, D_MODEL), lam.reshape(1, D_MODEL),
      w_out.astype(BF16), final_g.reshape(1, D_MODEL))


def kernel(x, c, positions, norm_g, w_mod, b_mod, attn_w_in, attn_w_out, lru_w_in, lru_conv_w,
           lru_conv_b, lru_w_a, lru_b_a, lru_w_x, lru_b_x, lru_lambda, lru_w_out, final_g):
    batch, seq, d_model = x.shape
    assert d_model == D_MODEL and DEPTH == 2
    assert batch % LRU_BATCH == 0 and seq % PROJ_ROWS == 0 and seq % MOBA_BLOCK == 0
    mod = _modulation(c, w_mod, b_mod)
    q_a, k_a, v_a, q_b, k_b, vt_b, g_a, g_b = _attn_projection(x, mod, norm_g, positions, attn_w_in[0])
    o_a = _sb_attention(q_a, k_a, v_a)
    o_b = _moba_attention(q_b, k_b, vt_b)
    return _lru_layer(x, mod, o_a, o_b, g_a, g_b, attn_w_out[0], norm_g, lru_w_in[0], lru_conv_w[0],
                      lru_conv_b[0], lru_w_a[0], lru_b_a[0], lru_w_x[0], lru_b_x[0], lru_lambda[0],
                      lru_w_out[0], final_g)
```

```python
import functools

import jax
import jax.numpy as jnp
import numpy as np
from jax import lax
from jax.experimental import pallas as pl
from jax.experimental.pallas import tpu as pltpu

F32 = jnp.float32
BF16 = jnp.bfloat16

D_MODEL = 1024
DEPTH = 2
HEAD_DIM = 64
N_HEADS = 8
GROUP_WIDTH = N_HEADS * HEAD_DIM
LANES = 128
HEAD_PAIRS = GROUP_WIDTH // LANES
SB_BLOCK = 128
MOBA_BLOCK = 256
MOBA_TOPK = 3
ROPE_THETA = 500000.0
ROT_DIM = HEAD_DIM // 4
LRU_BLOCKS = 8
LRU_BLOCK_WIDTH = D_MODEL // LRU_BLOCKS
CONV_WIDTH = 4
LRU_C = 8.0
EPS = 1e-6

PROJ_ROWS = 1024
PROJ_GROUP = 256
LRU_BATCH = 8
LRU_STEPS = 64
LRU_PITCH = LRU_STEPS + 8
VMEM_LIMIT = 56 * 1024 * 1024
F32_EXP_UNDERFLOW = -104.0
SB_SUB = 2
SB_DONE = 1e30
LOG2_E = 1.4426950408889634

_NT = (((1,), (1,)), ((), ()))


def _resident(shape, index_map):
    return pl.BlockSpec(shape, index_map, pipeline_mode=pl.Buffered(1))


def _rms(x, g):
    ms = jnp.mean(x * x, axis=-1, keepdims=True)
    return x * lax.rsqrt(ms + EPS) * g


def _mod_kernel(c_ref, w_ref, b_ref, o_ref):
    o_ref[...] = jnp.dot(c_ref[...], w_ref[...], preferred_element_type=F32,
                         precision=lax.Precision.HIGHEST) + b_ref[...]


def _modulation(c, w_mod, b_mod):
    batch = c.shape[0]
    return pl.pallas_call(
        _mod_kernel,
        grid=(DEPTH, 3),
        in_specs=[
            pl.BlockSpec((batch, D_MODEL), lambda l, j: (0, 0)),
            pl.BlockSpec((None, D_MODEL, D_MODEL), lambda l, j: (l, 0, j)),
            pl.BlockSpec((None, 1, D_MODEL), lambda l, j: (l, 0, j)),
        ],
        out_specs=pl.BlockSpec((None, batch, D_MODEL), lambda l, j: (l, 0, j)),
        out_shape=jax.ShapeDtypeStruct((DEPTH, batch, 3 * D_MODEL), F32),
        name="adaln_modulation",
    )(c, w_mod, b_mod.reshape(DEPTH, 1, 3 * D_MODEL))


def _proj_kernel(x_ref, mod_ref, g_ref, pos_ref, inv_ref, w_ref, wvt_ref,
                 qa_ref, ka_ref, va_ref, qb_ref, kb_ref, vbt_ref, ga_ref, gb_ref):
    shift = mod_ref[:, 0:D_MODEL]
    scale = mod_ref[:, D_MODEL:2 * D_MODEL]
    qk_scale = HEAD_DIM ** -0.5
    half = ROT_DIM // 2
    in_head = lax.broadcasted_iota(jnp.int32, (1, LANES), 1) % HEAD_DIM
    from_below = jnp.logical_and(in_head >= half, in_head < ROT_DIM)
    from_above = in_head < half

    for rows in (slice(r, r + PROJ_GROUP) for r in range(0, PROJ_ROWS, PROJ_GROUP)):
        h = (_rms(x_ref[rows, :], g_ref[...]) * (1.0 + scale) + shift).astype(BF16)

        def proj(j):
            return jnp.dot(h, w_ref[:, j * GROUP_WIDTH:(j + 1) * GROUP_WIDTH],
                           preferred_element_type=F32)

        ang = pos_ref[rows, :].astype(F32) * inv_ref[...]
        cos, sin = jnp.cos(ang), jnp.sin(ang)
        sin_from_below = jnp.where(from_below, sin, 0.0)
        sin_from_above = jnp.where(from_above, -sin, 0.0)

        def rotary(t):
            out = []
            for p in range(HEAD_PAIRS):
                tp = t[:, p * LANES:(p + 1) * LANES]
                out.append(tp * cos + pltpu.roll(tp, half, 1) * sin_from_below
                           + pltpu.roll(tp, LANES - half, 1) * sin_from_above)
            return jnp.concatenate(out, axis=1)

        qb_ref[rows, :] = (rotary(proj(3)) * (qk_scale * LOG2_E)).astype(BF16)
        kb_ref[rows, :] = rotary(proj(4)).astype(BF16)
        qa_ref[rows, :] = (proj(0) * qk_scale).astype(BF16)
        ka_ref[rows, :] = proj(1).astype(BF16)
        va_ref[rows, :] = proj(2).astype(BF16)
        vbt_ref[:, rows] = lax.dot_general(wvt_ref[...], h, _NT, preferred_element_type=F32).astype(BF16)
        ga_ref[rows, :] = jax.nn.silu(proj(5)).astype(BF16)
        gb_ref[rows, :] = jax.nn.silu(proj(6)).astype(BF16)


def _rope_inv_lanes():
    half = ROT_DIM // 2
    inv = ROPE_THETA ** (-np.arange(0, ROT_DIM, 2, dtype=np.float32) / ROT_DIM)
    per_head = np.zeros((HEAD_DIM,), np.float32)
    per_head[:half] = inv
    per_head[half:ROT_DIM] = inv
    return jnp.asarray(np.tile(per_head, LANES // HEAD_DIM).reshape(1, LANES))


def _attn_projection(x, mod, norm_g, positions, w_in):
    batch, seq, _ = x.shape
    gw = GROUP_WIDTH
    cols = lambda j: w_in[:, j * gw:(j + 1) * gw]
    w_all = jnp.concatenate(
        [cols(0), cols(1), cols(2), cols(3), cols(4), cols(6), cols(7)], axis=1).astype(BF16)
    w_vbt = cols(5).T.astype(BF16)
    n_cols = w_all.shape[1]
    out_spec = pl.BlockSpec((None, PROJ_ROWS, gw), lambda b, i: (b, i, 0))
    out_shape = jax.ShapeDtypeStruct((batch, seq, gw), BF16)
    out_specs = [out_spec] * 8
    out_shapes = [out_shape] * 8
    out_specs[5] = pl.BlockSpec((None, gw, PROJ_ROWS), lambda b, i: (b, 0, i))
    out_shapes[5] = jax.ShapeDtypeStruct((batch, gw, seq), BF16)
    return pl.pallas_call(
        _proj_kernel,
        grid=(batch, seq // PROJ_ROWS),
        in_specs=[
            pl.BlockSpec((None, PROJ_ROWS, D_MODEL), lambda b, i: (b, i, 0)),
            pl.BlockSpec((None, None, 1, 3 * D_MODEL), lambda b, i: (0, b, 0, 0)),
            pl.BlockSpec((None, 1, D_MODEL), lambda b, i: (0, 0, 0)),
            pl.BlockSpec((None, PROJ_ROWS, 1), lambda b, i: (b, i, 0)),
            pl.BlockSpec((1, LANES), lambda b, i: (0, 0)),
            _resident((D_MODEL, n_cols), lambda b, i: (0, 0)),
            _resident((gw, D_MODEL), lambda b, i: (0, 0)),
        ],
        out_specs=out_specs,
        out_shape=out_shapes,
        compiler_params=pltpu.CompilerParams(
            dimension_semantics=("parallel", "parallel"), vmem_limit_bytes=VMEM_LIMIT),
        name="attn_in_projection",
    )(x, mod.reshape(DEPTH, batch, 1, 3 * D_MODEL), norm_g.reshape(DEPTH, 1, D_MODEL),
      positions.reshape(batch, seq, 1), _rope_inv_lanes(), w_all, w_vbt)


def _head_lane_masks():
    lane = lax.broadcasted_iota(jnp.int32, (1, LANES), 1)
    return lane < HEAD_DIM, lane >= HEAD_DIM


def _split_heads_rows(x, lo_mask, hi_mask):
    zero = jnp.zeros_like(x)
    return jnp.concatenate([jnp.where(lo_mask, x, zero), jnp.where(hi_mask, x, zero)], axis=0)


def _sb_kernel(q_ref, k_ref, v_ref, o_ref, z_ref, logb_ref, later_ref, total_ref, run_ref, acc_ref):
    qt = pl.program_id(1)
    blk = SB_BLOCK
    lo_mask, hi_mask = _head_lane_masks()
    row = lax.broadcasted_iota(jnp.int32, (blk, 2 * blk), 0)
    col = lax.broadcasted_iota(jnp.int32, (blk, 2 * blk), 1)
    below_diag = jnp.where(col < blk, col, col - blk) < row
    r = lax.broadcasted_iota(jnp.int32, (blk, blk), 0)
    c = lax.broadcasted_iota(jnp.int32, (blk, blk), 1)
    later_and_total = jnp.concatenate([(r > c).astype(BF16), jnp.ones((blk, blk), BF16)], axis=1)
    later_and_total = jnp.concatenate([later_and_total, later_and_total], axis=0)
    tiles = [(g, p) for g in range(SB_SUB) for p in range(HEAD_PAIRS)]
    q_tiles = [q_ref[g * blk:(g + 1) * blk, p * LANES:(p + 1) * LANES] for g, p in tiles]

    def key_start(g, j):
        return pl.multiple_of(jnp.maximum(qt * SB_SUB + g - j, 0) * blk, blk)

    def scores(j):
        for t, (g, p) in enumerate(tiles):
            k_pair = k_ref[pl.ds(key_start(g, j), blk), p * LANES:(p + 1) * LANES]
            k2 = _split_heads_rows(k_pair, lo_mask, hi_mask)
            z_ref[t] = lax.dot_general(q_tiles[t], k2, _NT, preferred_element_type=F32)

    def sweep(j, diagonal):
        for t in range(len(tiles)):
            z = z_ref[t]
            soft = jnp.maximum(z, 0.0) + jnp.log(1.0 + jnp.exp2(jnp.abs(z) * -LOG2_E))
            logb_ref[t] = z - soft
            if diagonal:
                soft = jnp.where(below_diag, soft, 0.0)
            hi = soft.astype(BF16)
            lo = (soft - hi.astype(F32)).astype(BF16)
            for h in range(2):
                split = jnp.concatenate([hi[:, h * blk:(h + 1) * blk], lo[:, h * blk:(h + 1) * blk]], axis=1)
                sums = jnp.dot(split, later_and_total, preferred_element_type=F32)
                later_ref[t, :, h * blk:(h + 1) * blk] = sums[:, :blk]
                total_ref[t, :, h * blk:(h + 1) * blk] = sums[:, blk:]
        scores(j + 1)
        least = None
        for t, (g, p) in enumerate(tiles):
            run = run_ref[t]
            w = jnp.exp(logb_ref[t] - later_ref[t] - run)
            if diagonal:
                w = jnp.where(below_diag, w, 0.0)
            v_pair = v_ref[pl.ds(key_start(g, j), blk), p * LANES:(p + 1) * LANES]
            acc_ref[t] += jnp.dot(w.astype(BF16), _split_heads_rows(v_pair, lo_mask, hi_mask),
                                  preferred_element_type=F32)
            no_more_keys = qt * SB_SUB + g - j <= 0
            run = run + total_ref[t] + jnp.where(no_more_keys, SB_DONE, 0.0)
            run_ref[t] = run
            least = run if least is None else jnp.minimum(least, run)
        return jnp.min(least)

    acc_ref[...] = jnp.zeros_like(acc_ref)
    run_ref[...] = jnp.zeros_like(run_ref)
    scores(0)
    least = sweep(0, diagonal=True)

    def more(carry):
        _, least = carry
        return least < -F32_EXP_UNDERFLOW

    def next_sweep(carry):
        j, _ = carry
        return j + 1, sweep(j, diagonal=False)

    lax.while_loop(more, next_sweep, (jnp.int32(1), least))
    for t, (g, p) in enumerate(tiles):
        o_ref[g * blk:(g + 1) * blk, p * LANES:(p + 1) * LANES] = acc_ref[t].astype(o_ref.dtype)


def _sb_attention(q, k, v):
    batch, seq, width = q.shape
    rows = SB_SUB * SB_BLOCK
    n_tiles = SB_SUB * HEAD_PAIRS
    kv_spec = pl.BlockSpec((None, seq, width), lambda b, i: (b, 0, 0))
    q_spec = pl.BlockSpec((None, rows, width), lambda b, i: (b, i, 0))
    return pl.pallas_call(
        _sb_kernel,
        grid=(batch, seq // rows),
        in_specs=[q_spec, kv_spec, kv_spec],
        out_specs=q_spec,
        out_shape=jax.ShapeDtypeStruct(q.shape, BF16),
        scratch_shapes=[pltpu.VMEM((n_tiles, SB_BLOCK, 2 * SB_BLOCK), F32)] * 5
        + [pltpu.VMEM((n_tiles, SB_BLOCK, LANES), F32)],
        compiler_params=pltpu.CompilerParams(
            dimension_semantics=("parallel", "arbitrary"), vmem_limit_bytes=VMEM_LIMIT),
        name="stick_breaking_attention",
    )(q, k, v)


def _moba_kernel(q_ref, k_ref, vt_ref, o_ref, kmean_ref, sel_ref, s_ref, *, n_blocks):
    qi = pl.program_id(1)
    blk = MOBA_BLOCK
    ones_rows = 16

    @pl.when(qi == 0)
    def _():
        for n in range(n_blocks):
            kn = k_ref[n * blk:(n + 1) * blk, :].astype(F32)
            kmean_ref[n:n + 1, :] = jnp.mean(kn, axis=0, keepdims=True)

    lo_mask, hi_mask = _head_lane_masks()
    blk_row = lax.broadcasted_iota(jnp.int32, (n_blocks, blk), 0)
    blk_row_f = blk_row.astype(F32)
    past = blk_row < qi
    key_row = lax.broadcasted_iota(jnp.int32, (blk, blk), 0)
    query_col = lax.broadcasted_iota(jnp.int32, (blk, blk), 1)
    causal = key_row <= query_col

    q_heads = []
    for p in range(HEAD_PAIRS):
        q_pair = q_ref[:, p * LANES:(p + 1) * LANES]
        kmean = kmean_ref[:, p * LANES:(p + 1) * LANES].astype(BF16)
        for mask in (lo_mask, hi_mask):
            qh = jnp.where(mask, q_pair, jnp.zeros_like(q_pair))
            gate = lax.dot_general(kmean, qh, _NT, preferred_element_type=F32)
            gate = jnp.where(past, gate, -jnp.inf)
            selected = jnp.zeros((n_blocks, blk), F32)
            for _ in range(MOBA_TOPK):
                best = jnp.max(gate, axis=0, keepdims=True)
                first = jnp.min(jnp.where(gate == best, blk_row_f, float(n_blocks)), axis=0, keepdims=True)
                hit = blk_row_f == first
                selected = jnp.where(jnp.logical_and(hit, past), 1.0, selected)
                gate = jnp.where(hit, -jnp.inf, gate)
            sel_ref[len(q_heads)] = selected
            q_heads.append(qh)

    def head_block(h, kb, slot, state, vt_pair, diagonal):
        m, acc = state
        s = s_ref[slot, h]
        if diagonal:
            s = jnp.where(causal, s, -jnp.inf)
            m_new = jnp.maximum(m, jnp.max(s, axis=0, keepdims=True))
            shift = m_new
        else:
            picked = jnp.logical_and(sel_ref[h, pl.ds(jnp.maximum(kb, 0), 1), :] > 0.5, kb >= 0)
            m_new = jnp.where(picked, jnp.maximum(m, jnp.max(s, axis=0, keepdims=True)), m)
            shift = jnp.where(picked, m_new, jnp.inf)
        p = jnp.exp2(s - shift).astype(BF16)
        alpha = jnp.exp2(m - m_new)
        sub = (h % 2) * HEAD_DIM
        vt_aug = jnp.concatenate([vt_pair[sub:sub + HEAD_DIM, :], jnp.ones((ones_rows, blk), BF16)], axis=0)
        acc = alpha * acc + jnp.dot(vt_aug, p, preferred_element_type=F32)
        return m_new, acc

    def score_stage(kb, slot):
        start = pl.multiple_of(jnp.maximum(kb, 0) * blk, blk)
        for p in range(HEAD_PAIRS):
            k_pair = k_ref[pl.ds(start, blk), p * LANES:(p + 1) * LANES]
            for h in (2 * p, 2 * p + 1):
                s_ref[slot, h] = lax.dot_general(k_pair, q_heads[h], _NT, preferred_element_type=F32)

    def value_stage(kb, slot, states, diagonal):
        start = pl.multiple_of(jnp.maximum(kb, 0) * blk, blk)
        new = []
        for p in range(HEAD_PAIRS):
            vt_pair = vt_ref[p * LANES:(p + 1) * LANES, pl.ds(start, blk)]
            for h in (2 * p, 2 * p + 1):
                new.append(head_block(h, kb, slot, states[h], vt_pair, diagonal))
        return tuple(new)

    init = (jnp.full((1, blk), -jnp.inf, F32), jnp.zeros((HEAD_DIM + ones_rows, blk), F32))
    score_stage(qi, 0)
    score_stage(qi - 1, 1)
    states = value_stage(qi, 0, (init,) * N_HEADS, diagonal=True)

    def two_blocks(i, states):
        kb = qi - 1 - 2 * i
        score_stage(kb - 1, 0)
        states = value_stage(kb, 1, states, diagonal=False)
        score_stage(kb - 2, 1)
        return value_stage(kb - 1, 0, states, diagonal=False)

    states = lax.fori_loop(0, (qi + 1) // 2, two_blocks, states)
    for p in range(HEAD_PAIRS):
        outs = []
        for h in (2 * p, 2 * p + 1):
            acc = states[h][1]
            outs.append(acc[0:HEAD_DIM, :] / acc[HEAD_DIM:HEAD_DIM + 1, :])
        o_ref[:, p * LANES:(p + 1) * LANES] = jnp.concatenate(outs, axis=0).T.astype(o_ref.dtype)


def _moba_attention(q, k, vt):
    batch, seq, width = q.shape
    n_blocks = seq // MOBA_BLOCK
    q_spec = pl.BlockSpec((None, MOBA_BLOCK, width), lambda b, i: (b, i, 0))
    return pl.pallas_call(
        functools.partial(_moba_kernel, n_blocks=n_blocks),
        grid=(batch, n_blocks),
        in_specs=[q_spec,
                  pl.BlockSpec((None, seq, width), lambda b, i: (b, 0, 0)),
                  pl.BlockSpec((None, width, seq), lambda b, i: (b, 0, 0))],
        out_specs=q_spec,
        out_shape=jax.ShapeDtypeStruct(q.shape, BF16),
        scratch_shapes=[pltpu.VMEM((n_blocks, width), F32),
                        pltpu.VMEM((N_HEADS, n_blocks, MOBA_BLOCK), F32),
                        pltpu.VMEM((2, N_HEADS, MOBA_BLOCK, MOBA_BLOCK), F32)],
        compiler_params=pltpu.CompilerParams(
            dimension_semantics=("parallel", "arbitrary"), vmem_limit_bytes=VMEM_LIMIT),
        name="moba_attention",
    )(q, k, vt)


def _lru_kernel(x_ref, mod0_ref, oa_ref, ob_ref, ga_ref, gb_ref, w_attn_ref,
                mod_ref, g_ref, w_in_ref, conv_w_ref, conv_b_ref, w_gate_ref,
                b_a_ref, b_x_ref, lam_ref, w_out_ref, final_g_ref, o_ref,
                x1_ref, h_ref, xb_ref, xc_ref, sg_ref, a_ref, b_ref, y_ref, state_ref):
    step_i = pl.program_id(1)
    nb, nt, pitch = LRU_BATCH, LRU_STEPS, LRU_PITCH
    width = LRU_BLOCK_WIDTH

    @pl.when(step_i == 0)
    def _():
        state_ref[...] = jnp.zeros_like(state_ref)
        xb_ref[:, 0:8, :] = jnp.zeros((nb, 8, D_MODEL), F32)

    for b in range(nb):
        y_ref[b * nt:(b + 1) * nt, 0:GROUP_WIDTH] = oa_ref[b] * ga_ref[b]
        y_ref[b * nt:(b + 1) * nt, GROUP_WIDTH:2 * GROUP_WIDTH] = ob_ref[b] * gb_ref[b]
    y0 = jnp.dot(y_ref[...], w_attn_ref[...], preferred_element_type=F32)

    for b in range(nb):
        x1 = x_ref[b] + mod0_ref[b, :, 2 * D_MODEL:3 * D_MODEL] * y0[b * nt:(b + 1) * nt, :]
        x1_ref[b] = x1
        shift = mod_ref[b, :, 0:D_MODEL]
        scale = mod_ref[b, :, D_MODEL:2 * D_MODEL]
        h = _rms(x1, g_ref[...]) * (1.0 + scale) + shift
        h_ref[b * nt:(b + 1) * nt, :] = h.astype(BF16)

    xb = jnp.dot(h_ref[...], w_in_ref[:, 0:D_MODEL], preferred_element_type=F32)
    for b in range(nb):
        xb_ref[b, 8:8 + nt, :] = xb[b * nt:(b + 1) * nt, :]
    gate_branch = jnp.dot(h_ref[...], w_in_ref[:, D_MODEL:2 * D_MODEL], preferred_element_type=F32)
    sg_ref[...] = jax.nn.silu(gate_branch)

    for b in range(nb):
        xc = conv_b_ref[...] + conv_w_ref[CONV_WIDTH - 1:CONV_WIDTH, :] * xb_ref[b, 8:8 + nt, :]
        for tap in range(CONV_WIDTH - 1):
            off = 8 - (CONV_WIDTH - 1) + tap
            xc = xc + conv_w_ref[tap:tap + 1, :] * xb_ref[b, off:off + nt, :]
        xc_ref[b * nt:(b + 1) * nt, :] = xc
        xb_ref[b, 0:8, :] = xb_ref[b, nt:nt + 8, :]

    for n in range(LRU_BLOCKS):
        cols = slice(n * width, (n + 1) * width)
        xc_n = xc_ref[:, cols]
        gates = jnp.dot(xc_n.astype(BF16), w_gate_ref[n], preferred_element_type=F32)
        r = jax.nn.sigmoid(gates[:, 0:width] + b_a_ref[:, cols])
        inp = jax.nn.sigmoid(gates[:, width:2 * width] + b_x_ref[:, cols])
        log_a = LRU_C * r * jax.nn.log_sigmoid(lam_ref[:, cols])
        a = jnp.exp(log_a)
        gain_sq = -jnp.tanh(log_a) * (a * a + 1.0)
        mult = jnp.where(gain_sq > 0.0, gain_sq * lax.rsqrt(gain_sq), 0.0)
        b_term = mult * (inp * xc_n)
        for b in range(nb):
            a_ref[n, b * pitch:b * pitch + nt, :] = a[b * nt:(b + 1) * nt, :]
            b_ref[n, b * pitch:b * pitch + nt, :] = b_term[b * nt:(b + 1) * nt, :]

    def scan_step(t, hs):
        new = []
        for n in range(LRU_BLOCKS):
            rows = pl.ds(t, nb, stride=pitch)
            h_n = a_ref[n, rows, :] * hs[n] + b_ref[n, rows, :]
            b_ref[n, rows, :] = h_n
            new.append(h_n)
        return tuple(new)

    hs = lax.fori_loop(0, nt, scan_step, tuple(state_ref[n] for n in range(LRU_BLOCKS)), unroll=4)
    for n in range(LRU_BLOCKS):
        state_ref[n] = hs[n]

    for b in range(nb):
        hs_b = jnp.concatenate(
            [b_ref[n, b * pitch:b * pitch + nt, :] for n in range(LRU_BLOCKS)], axis=1)
        y_ref[b * nt:(b + 1) * nt, :] = (hs_b * sg_ref[b * nt:(b + 1) * nt, :]).astype(BF16)

    y = jnp.dot(y_ref[...], w_out_ref[...], preferred_element_type=F32)
    for b in range(nb):
        gate = mod_ref[b, :, 2 * D_MODEL:3 * D_MODEL]
        x_new = x1_ref[b] + gate * y[b * nt:(b + 1) * nt, :]
        o_ref[b] = _rms(x_new, final_g_ref[...])


def _lru_layer(x, mod, o_a, o_b, g_a, g_b, w_attn_out, norm_g, w_in, conv_w, conv_b, w_a, b_a, w_x, b_x,
               lam, w_out, final_g):
    batch, seq, _ = x.shape
    rows = LRU_BATCH * LRU_STEPS
    x_spec = pl.BlockSpec((LRU_BATCH, LRU_STEPS, D_MODEL), lambda g, i: (g, i, 0))
    half_spec = pl.BlockSpec((LRU_BATCH, LRU_STEPS, GROUP_WIDTH), lambda g, i: (g, i, 0))
    vec_spec = pl.BlockSpec((1, D_MODEL), lambda g, i: (0, 0))
    mod4 = mod.reshape(DEPTH, batch, 1, 3 * D_MODEL)
    w_gate = jnp.concatenate([w_a, w_x], axis=-1).astype(BF16)
    return pl.pallas_call(
        _lru_kernel,
        grid=(batch // LRU_BATCH, seq // LRU_STEPS),
        in_specs=[
            x_spec,
            pl.BlockSpec((None, LRU_BATCH, 1, 3 * D_MODEL), lambda g, i: (0, g, 0, 0)),
            half_spec, half_spec, half_spec, half_spec,
            _resident((D_MODEL, D_MODEL), lambda g, i: (0, 0)),
            pl.BlockSpec((None, LRU_BATCH, 1, 3 * D_MODEL), lambda g, i: (1, g, 0, 0)),
            pl.BlockSpec((None, 1, D_MODEL), lambda g, i: (1, 0, 0)),
            _resident((D_MODEL, 2 * D_MODEL), lambda g, i: (0, 0)),
            pl.BlockSpec((CONV_WIDTH, D_MODEL), lambda g, i: (0, 0)),
            vec_spec,
            _resident((LRU_BLOCKS, LRU_BLOCK_WIDTH, 2 * LRU_BLOCK_WIDTH), lambda g, i: (0, 0, 0)),
            vec_spec, vec_spec, vec_spec,
            _resident((D_MODEL, D_MODEL), lambda g, i: (0, 0)),
            vec_spec,
        ],
        out_specs=x_spec,
        out_shape=jax.ShapeDtypeStruct(x.shape, F32),
        scratch_shapes=[
            pltpu.VMEM((LRU_BATCH, LRU_STEPS, D_MODEL), F32),
            pltpu.VMEM((rows, D_MODEL), BF16),
            pltpu.VMEM((LRU_BATCH, 8 + LRU_STEPS, D_MODEL), F32),
            pltpu.VMEM((rows, D_MODEL), F32),
            pltpu.VMEM((rows, D_MODEL), F32),
            pltpu.VMEM((LRU_BLOCKS, LRU_BATCH * LRU_PITCH, LRU_BLOCK_WIDTH), F32),
            pltpu.VMEM((LRU_BLOCKS, LRU_BATCH * LRU_PITCH, LRU_BLOCK_WIDTH), F32),
            pltpu.VMEM((rows, D_MODEL), BF16),
            pltpu.VMEM((LRU_BLOCKS, LRU_BATCH, LRU_BLOCK_WIDTH), F32),
        ],
        compiler_params=pltpu.CompilerParams(
            dimension_semantics=("parallel", "arbitrary"), vmem_limit_bytes=VMEM_LIMIT),
        name="rglru_layer_final_norm",
    )(x, mod4, o_a, o_b, g_a, g_b, w_attn_out.astype(BF16), mod4, norm_g.reshape(DEPTH, 1, D_MODEL),
      w_in.astype(BF16), conv_w, conv_b.reshape(1, D_MODEL), w_gate,
      b_a.reshape(1, D_MODEL), b_x.reshape(1, D_MODEL), lam.reshape(1, D_MODEL),
      w_out.astype(BF16), final_g.reshape(1, D_MODEL))


def kernel(x, c, positions, norm_g, w_mod, b_mod, attn_w_in, attn_w_out, lru_w_in, lru_conv_w,
           lru_conv_b, lru_w_a, lru_b_a, lru_w_x, lru_b_x, lru_lambda, lru_w_out, final_g):
    batch, seq, d_model = x.shape
    assert d_model == D_MODEL and DEPTH == 2
    assert batch % LRU_BATCH == 0 and seq % PROJ_ROWS == 0 and seq % MOBA_BLOCK == 0
    mod = _modulation(c, w_mod, b_mod)
    q_a, k_a, v_a, q_b, k_b, vt_b, g_a, g_b = _attn_projection(x, mod, norm_g, positions, attn_w_in[0])
    o_a = _sb_attention(q_a, k_a, v_a)
    o_b = _moba_attention(q_b, k_b, vt_b)
    return _lru_layer(x, mod, o_a, o_b, g_a, g_b, attn_w_out[0], norm_g, lru_w_in[0], lru_conv_w[0],
                      lru_conv_b[0], lru_w_a[0], lru_b_a[0], lru_w_x[0], lru_b_x[0], lru_lambda[0],
                      lru_w_out[0], final_g)
```

```python
import functools

import jax
import jax.numpy as jnp
import numpy as np
from jax import lax
from jax.experimental import pallas as pl
from jax.experimental.pallas import tpu as pltpu

F32 = jnp.float32
BF16 = jnp.bfloat16

D_MODEL = 1024
DEPTH = 2
HEAD_DIM = 64
N_HEADS = 8
GROUP_WIDTH = N_HEADS * HEAD_DIM
LANES = 128
SUBLANES = 8
BF16_SUBLANES = 16
HEAD_PAIRS = GROUP_WIDTH // LANES
SB_BLOCK = 128
MOBA_BLOCK = 256
MOBA_TOPK = 3
ROPE_THETA = 500000.0
ROT_DIM = HEAD_DIM // 4
LRU_BLOCKS = 8
LRU_BLOCK_WIDTH = D_MODEL // LRU_BLOCKS
CONV_WIDTH = 4
LRU_C = 8.0
EPS = 1e-6

PROJ_ROWS = 1024
PROJ_GROUP = 256
LRU_BATCH = SUBLANES
LRU_STEPS = 64
LRU_PITCH = LRU_STEPS + SUBLANES
CONV_HISTORY = SUBLANES
VMEM_LIMIT = 56 * 1024 * 1024
F32_EXP_UNDERFLOW = -104.0
SB_SUB = 2
SB_DONE = 1e30
LOG2_E = 1.4426950408889634

_NT = (((1,), (1,)), ((), ()))


def _resident(shape, index_map):
    return pl.BlockSpec(shape, index_map, pipeline_mode=pl.Buffered(1))


def _rms(x, g):
    ms = jnp.mean(x * x, axis=-1, keepdims=True)
    return x * lax.rsqrt(ms + EPS) * g


def _mod_kernel(c_ref, w_ref, b_ref, o_ref):
    o_ref[...] = jnp.dot(c_ref[...], w_ref[...], preferred_element_type=F32,
                         precision=lax.Precision.HIGHEST) + b_ref[...]


def _modulation(c, w_mod, b_mod):
    batch = c.shape[0]
    return pl.pallas_call(
        _mod_kernel,
        grid=(DEPTH, 3),
        in_specs=[
            pl.BlockSpec((batch, D_MODEL), lambda l, j: (0, 0)),
            pl.BlockSpec((None, D_MODEL, D_MODEL), lambda l, j: (l, 0, j)),
            pl.BlockSpec((None, 1, D_MODEL), lambda l, j: (l, 0, j)),
        ],
        out_specs=pl.BlockSpec((None, batch, D_MODEL), lambda l, j: (l, 0, j)),
        out_shape=jax.ShapeDtypeStruct((DEPTH, batch, 3 * D_MODEL), F32),
        name="adaln_modulation",
    )(c, w_mod, b_mod.reshape(DEPTH, 1, 3 * D_MODEL))


def _proj_kernel(x_ref, mod_ref, g_ref, pos_ref, inv_ref, w_ref, wvt_ref,
                 qa_ref, ka_ref, va_ref, qb_ref, kb_ref, vbt_ref, ga_ref, gb_ref):
    shift = mod_ref[:, 0:D_MODEL]
    scale = mod_ref[:, D_MODEL:2 * D_MODEL]
    qk_scale = HEAD_DIM ** -0.5
    half = ROT_DIM // 2
    in_head = lax.broadcasted_iota(jnp.int32, (1, LANES), 1) % HEAD_DIM
    from_below = jnp.logical_and(in_head >= half, in_head < ROT_DIM)
    from_above = in_head < half
    first_head = lax.broadcasted_iota(jnp.int32, (1, GROUP_WIDTH), 1) % LANES < HEAD_DIM

    for rows in (slice(r, r + PROJ_GROUP) for r in range(0, PROJ_ROWS, PROJ_GROUP)):
        h = (_rms(x_ref[rows, :], g_ref[...]) * (1.0 + scale) + shift).astype(BF16)

        def proj(j):
            return jnp.dot(h, w_ref[:, j * GROUP_WIDTH:(j + 1) * GROUP_WIDTH],
                           preferred_element_type=F32)

        ang = pos_ref[rows, :].astype(F32) * inv_ref[...]
        cos, sin = jnp.cos(ang), jnp.sin(ang)
        sin_from_below = jnp.where(from_below, sin, 0.0)
        sin_from_above = jnp.where(from_above, -sin, 0.0)

        def rotary(t):
            out = []
            for p in range(HEAD_PAIRS):
                tp = t[:, p * LANES:(p + 1) * LANES]
                out.append(tp * cos + pltpu.roll(tp, half, 1) * sin_from_below
                           + pltpu.roll(tp, LANES - half, 1) * sin_from_above)
            return jnp.concatenate(out, axis=1)

        qb_ref[rows, :] = (rotary(proj(3)) * (qk_scale * LOG2_E)).astype(BF16)
        kb_ref[rows, :] = rotary(proj(4)).astype(BF16)
        qa_ref[rows, :] = (proj(0) * qk_scale).astype(BF16)
        for ref, t in ((ka_ref, proj(1).astype(BF16)), (va_ref, proj(2).astype(BF16))):
            head0 = jnp.where(first_head, t, jnp.zeros_like(t))
            head1 = jnp.where(first_head, jnp.zeros_like(t), t)
            for r in range(0, PROJ_GROUP, SB_BLOCK):
                dst = 2 * (rows.start + r)
                ref[dst:dst + SB_BLOCK, :] = head0[r:r + SB_BLOCK, :]
                ref[dst + SB_BLOCK:dst + 2 * SB_BLOCK, :] = head1[r:r + SB_BLOCK, :]
        vbt_ref[:, rows] = lax.dot_general(wvt_ref[...], h, _NT, preferred_element_type=F32).astype(BF16)
        ga_ref[rows, :] = jax.nn.silu(proj(5)).astype(BF16)
        gb_ref[rows, :] = jax.nn.silu(proj(6)).astype(BF16)


def _rope_inv_lanes():
    half = ROT_DIM // 2
    inv = ROPE_THETA ** (-np.arange(0, ROT_DIM, 2, dtype=np.float32) / ROT_DIM)
    per_head = np.zeros((HEAD_DIM,), np.float32)
    per_head[:half] = inv
    per_head[half:ROT_DIM] = inv
    return jnp.asarray(np.tile(per_head, LANES // HEAD_DIM).reshape(1, LANES))


def _attn_projection(x, mod, norm_g, positions, w_in):
    batch, seq, _ = x.shape
    gw = GROUP_WIDTH
    cols = lambda j: w_in[:, j * gw:(j + 1) * gw]
    w_all = jnp.concatenate(
        [cols(0), cols(1), cols(2), cols(3), cols(4), cols(6), cols(7)], axis=1).astype(BF16)
    w_vbt = cols(5).T.astype(BF16)
    n_cols = w_all.shape[1]
    out_spec = pl.BlockSpec((None, PROJ_ROWS, gw), lambda b, i: (b, i, 0))
    out_shape = jax.ShapeDtypeStruct((batch, seq, gw), BF16)
    out_specs = [out_spec] * 8
    out_shapes = [out_shape] * 8
    out_specs[5] = pl.BlockSpec((None, gw, PROJ_ROWS), lambda b, i: (b, 0, i))
    out_shapes[5] = jax.ShapeDtypeStruct((batch, gw, seq), BF16)
    for j in (1, 2):
        out_specs[j] = pl.BlockSpec((None, 2 * PROJ_ROWS, gw), lambda b, i: (b, i, 0))
        out_shapes[j] = jax.ShapeDtypeStruct((batch, 2 * seq, gw), BF16)
    return pl.pallas_call(
        _proj_kernel,
        grid=(batch, seq // PROJ_ROWS),
        in_specs=[
            pl.BlockSpec((None, PROJ_ROWS, D_MODEL), lambda b, i: (b, i, 0)),
            pl.BlockSpec((None, None, 1, 3 * D_MODEL), lambda b, i: (0, b, 0, 0)),
            pl.BlockSpec((None, 1, D_MODEL), lambda b, i: (0, 0, 0)),
            pl.BlockSpec((None, PROJ_ROWS, 1), lambda b, i: (b, i, 0)),
            pl.BlockSpec((1, LANES), lambda b, i: (0, 0)),
            _resident((D_MODEL, n_cols), lambda b, i: (0, 0)),
            _resident((gw, D_MODEL), lambda b, i: (0, 0)),
        ],
        out_specs=out_specs,
        out_shape=out_shapes,
        compiler_params=pltpu.CompilerParams(
            dimension_semantics=("parallel", "parallel"), vmem_limit_bytes=VMEM_LIMIT),
        name="attn_in_projection",
    )(x, mod.reshape(DEPTH, batch, 1, 3 * D_MODEL), norm_g.reshape(DEPTH, 1, D_MODEL),
      positions.reshape(batch, seq, 1), _rope_inv_lanes(), w_all, w_vbt)


def _head_lane_masks():
    lane = lax.broadcasted_iota(jnp.int32, (1, LANES), 1)
    return lane < HEAD_DIM, lane >= HEAD_DIM


def _sb_kernel(q_ref, k_ref, v_ref, o_ref, z_ref, logb_ref, later_ref, total_ref, run_ref, acc_ref):
    qt = pl.program_id(1)
    blk = SB_BLOCK
    row = lax.broadcasted_iota(jnp.int32, (blk, 2 * blk), 0)
    col = lax.broadcasted_iota(jnp.int32, (blk, 2 * blk), 1)
    below_diag = jnp.where(col < blk, col, col - blk) < row
    r = lax.broadcasted_iota(jnp.int32, (blk, blk), 0)
    c = lax.broadcasted_iota(jnp.int32, (blk, blk), 1)
    later_and_total = jnp.concatenate([(r > c).astype(BF16), jnp.ones((blk, blk), BF16)], axis=1)
    later_and_total = jnp.concatenate([later_and_total, later_and_total], axis=0)
    tiles = [(g, p) for g in range(SB_SUB) for p in range(HEAD_PAIRS)]
    q_tiles = [q_ref[g * blk:(g + 1) * blk, p * LANES:(p + 1) * LANES] for g, p in tiles]

    def key_rows(g, j):
        start = pl.multiple_of(jnp.maximum(qt * SB_SUB + g - j, 0) * (2 * blk), 2 * blk)
        return pl.ds(start, 2 * blk)

    def scores(j):
        for t, (g, p) in enumerate(tiles):
            k2 = k_ref[key_rows(g, j), p * LANES:(p + 1) * LANES]
            z_ref[t] = lax.dot_general(q_tiles[t], k2, _NT, preferred_element_type=F32)

    def sweep(j, diagonal):
        for t in range(len(tiles)):
            z = z_ref[t]
            soft = jnp.maximum(z, 0.0) + jnp.log(1.0 + jnp.exp2(jnp.abs(z) * -LOG2_E))
            logb_ref[t] = z - soft
            if diagonal:
                soft = jnp.where(below_diag, soft, 0.0)
            hi = soft.astype(BF16)
            lo = (soft - hi.astype(F32)).astype(BF16)
            for h in range(2):
                split = jnp.concatenate([hi[:, h * blk:(h + 1) * blk], lo[:, h * blk:(h + 1) * blk]], axis=1)
                sums = jnp.dot(split, later_and_total, preferred_element_type=F32)
                later_ref[t, :, h * blk:(h + 1) * blk] = sums[:, :blk]
                total_ref[t, :, h * blk:(h + 1) * blk] = sums[:, blk:]
        scores(j + 1)
        least = None
        for t, (g, p) in enumerate(tiles):
            run = run_ref[t]
            w = jnp.exp(logb_ref[t] - later_ref[t] - run)
            if diagonal:
                w = jnp.where(below_diag, w, 0.0)
            v2 = v_ref[key_rows(g, j), p * LANES:(p + 1) * LANES]
            acc_ref[t] += jnp.dot(w.astype(BF16), v2, preferred_element_type=F32)
            no_more_keys = qt * SB_SUB + g - j <= 0
            run = run + total_ref[t] + jnp.where(no_more_keys, SB_DONE, 0.0)
            run_ref[t] = run
            least = run if least is None else jnp.minimum(least, run)
        return jnp.min(least)

    acc_ref[...] = jnp.zeros_like(acc_ref)
    run_ref[...] = jnp.zeros_like(run_ref)
    scores(0)
    least = sweep(0, diagonal=True)

    def more(carry):
        _, least = carry
        return least < -F32_EXP_UNDERFLOW

    def next_sweep(carry):
        j, _ = carry
        return j + 1, sweep(j, diagonal=False)

    lax.while_loop(more, next_sweep, (jnp.int32(1), least))
    for t, (g, p) in enumerate(tiles):
        o_ref[g * blk:(g + 1) * blk, p * LANES:(p + 1) * LANES] = acc_ref[t].astype(o_ref.dtype)


def _sb_attention(q, k, v):
    batch, seq, width = q.shape
    rows = SB_SUB * SB_BLOCK
    n_tiles = SB_SUB * HEAD_PAIRS
    kv_spec = pl.BlockSpec((None, 2 * seq, width), lambda b, i: (b, 0, 0))
    q_spec = pl.BlockSpec((None, rows, width), lambda b, i: (b, i, 0))
    return pl.pallas_call(
        _sb_kernel,
        grid=(batch, seq // rows),
        in_specs=[q_spec, kv_spec, kv_spec],
        out_specs=q_spec,
        out_shape=jax.ShapeDtypeStruct(q.shape, BF16),
        scratch_shapes=[pltpu.VMEM((n_tiles, SB_BLOCK, 2 * SB_BLOCK), F32)] * 5
        + [pltpu.VMEM((n_tiles, SB_BLOCK, LANES), F32)],
        compiler_params=pltpu.CompilerParams(
            dimension_semantics=("parallel", "arbitrary"), vmem_limit_bytes=VMEM_LIMIT),
        name="stick_breaking_attention",
    )(q, k, v)


def _moba_kernel(q_ref, k_ref, vt_ref, o_ref, kmean_ref, sel_ref, s_ref, *, n_blocks):
    qi = pl.program_id(1)
    blk = MOBA_BLOCK
    ones_rows = BF16_SUBLANES

    @pl.when(qi == 0)
    def _():
        for n in range(n_blocks):
            kn = k_ref[n * blk:(n + 1) * blk, :].astype(F32)
            kmean_ref[n:n + 1, :] = jnp.mean(kn, axis=0, keepdims=True)

    lo_mask, hi_mask = _head_lane_masks()
    blk_row = lax.broadcasted_iota(jnp.int32, (n_blocks, blk), 0)
    blk_row_f = blk_row.astype(F32)
    past = blk_row < qi
    key_row = lax.broadcasted_iota(jnp.int32, (blk, blk), 0)
    query_col = lax.broadcasted_iota(jnp.int32, (blk, blk), 1)
    causal = key_row <= query_col

    q_heads = []
    for p in range(HEAD_PAIRS):
        q_pair = q_ref[:, p * LANES:(p + 1) * LANES]
        kmean = kmean_ref[:, p * LANES:(p + 1) * LANES].astype(BF16)
        for mask in (lo_mask, hi_mask):
            qh = jnp.where(mask, q_pair, jnp.zeros_like(q_pair))
            gate = lax.dot_general(kmean, qh, _NT, preferred_element_type=F32)
            gate = jnp.where(past, gate, -jnp.inf)
            selected = jnp.zeros((n_blocks, blk), F32)
            for _ in range(MOBA_TOPK):
                best = jnp.max(gate, axis=0, keepdims=True)
                first = jnp.min(jnp.where(gate == best, blk_row_f, float(n_blocks)), axis=0, keepdims=True)
                hit = blk_row_f == first
                selected = jnp.where(jnp.logical_and(hit, past), 1.0, selected)
                gate = jnp.where(hit, -jnp.inf, gate)
            sel_ref[len(q_heads)] = selected
            q_heads.append(qh)

    def head_block(h, kb, slot, state, vt_pair, diagonal):
        m, acc = state
        s = s_ref[slot, h]
        if diagonal:
            s = jnp.where(causal, s, -jnp.inf)
            m_new = jnp.maximum(m, jnp.max(s, axis=0, keepdims=True))
            shift = m_new
        else:
            picked = jnp.logical_and(sel_ref[h, pl.ds(jnp.maximum(kb, 0), 1), :] > 0.5, kb >= 0)
            m_new = jnp.where(picked, jnp.maximum(m, jnp.max(s, axis=0, keepdims=True)), m)
            shift = jnp.where(picked, m_new, jnp.inf)
        p = jnp.exp2(s - shift).astype(BF16)
        alpha = jnp.exp2(m - m_new)
        sub = (h % 2) * HEAD_DIM
        vt_aug = jnp.concatenate([vt_pair[sub:sub + HEAD_DIM, :], jnp.ones((ones_rows, blk), BF16)], axis=0)
        acc = alpha * acc + jnp.dot(vt_aug, p, preferred_element_type=F32)
        return m_new, acc

    def score_stage(kb, slot):
        start = pl.multiple_of(jnp.maximum(kb, 0) * blk, blk)
        for p in range(HEAD_PAIRS):
            k_pair = k_ref[pl.ds(start, blk), p * LANES:(p + 1) * LANES]
            for h in (2 * p, 2 * p + 1):
                s_ref[slot, h] = lax.dot_general(k_pair, q_heads[h], _NT, preferred_element_type=F32)

    def value_stage(kb, slot, states, diagonal):
        start = pl.multiple_of(jnp.maximum(kb, 0) * blk, blk)
        new = []
        for p in range(HEAD_PAIRS):
            vt_pair = vt_ref[p * LANES:(p + 1) * LANES, pl.ds(start, blk)]
            for h in (2 * p, 2 * p + 1):
                new.append(head_block(h, kb, slot, states[h], vt_pair, diagonal))
        return tuple(new)

    init = (jnp.full((1, blk), -jnp.inf, F32), jnp.zeros((HEAD_DIM + ones_rows, blk), F32))
    score_stage(qi, 0)
    score_stage(qi - 1, 1)
    states = value_stage(qi, 0, (init,) * N_HEADS, diagonal=True)

    def two_blocks(i, states):
        kb = qi - 1 - 2 * i
        score_stage(kb - 1, 0)
        states = value_stage(kb, 1, states, diagonal=False)
        score_stage(kb - 2, 1)
        return value_stage(kb - 1, 0, states, diagonal=False)

    states = lax.fori_loop(0, (qi + 1) // 2, two_blocks, states)
    for p in range(HEAD_PAIRS):
        outs = []
        for h in (2 * p, 2 * p + 1):
            acc = states[h][1]
            outs.append(acc[0:HEAD_DIM, :] / acc[HEAD_DIM:HEAD_DIM + 1, :])
        o_ref[:, p * LANES:(p + 1) * LANES] = jnp.concatenate(outs, axis=0).T.astype(o_ref.dtype)


def _moba_attention(q, k, vt):
    batch, seq, width = q.shape
    n_blocks = seq // MOBA_BLOCK
    q_spec = pl.BlockSpec((None, MOBA_BLOCK, width), lambda b, i: (b, i, 0))
    return pl.pallas_call(
        functools.partial(_moba_kernel, n_blocks=n_blocks),
        grid=(batch, n_blocks),
        in_specs=[q_spec,
                  pl.BlockSpec((None, seq, width), lambda b, i: (b, 0, 0)),
                  pl.BlockSpec((None, width, seq), lambda b, i: (b, 0, 0))],
        out_specs=q_spec,
        out_shape=jax.ShapeDtypeStruct(q.shape, BF16),
        scratch_shapes=[pltpu.VMEM((n_blocks, width), F32),
                        pltpu.VMEM((N_HEADS, n_blocks, MOBA_BLOCK), F32),
                        pltpu.VMEM((2, N_HEADS, MOBA_BLOCK, MOBA_BLOCK), F32)],
        compiler_params=pltpu.CompilerParams(
            dimension_semantics=("parallel", "arbitrary"), vmem_limit_bytes=VMEM_LIMIT),
        name="moba_attention",
    )(q, k, vt)


def _lru_kernel(x_ref, mod0_ref, oa_ref, ob_ref, ga_ref, gb_ref, w_attn_ref,
                mod_ref, g_ref, w_in_ref, conv_w_ref, conv_b_ref, w_gate_ref,
                b_a_ref, b_x_ref, lam_ref, w_out_ref, final_g_ref, o_ref,
                x1_ref, h_ref, xb_ref, xc_ref, sg_ref, a_ref, b_ref, y_ref, state_ref):
    step_i = pl.program_id(1)
    nb, nt, pitch, hist = LRU_BATCH, LRU_STEPS, LRU_PITCH, CONV_HISTORY
    width = LRU_BLOCK_WIDTH

    @pl.when(step_i == 0)
    def _():
        state_ref[...] = jnp.zeros_like(state_ref)
        xb_ref[:, 0:hist, :] = jnp.zeros((nb, hist, D_MODEL), F32)

    for b in range(nb):
        y_ref[b * nt:(b + 1) * nt, 0:GROUP_WIDTH] = oa_ref[b] * ga_ref[b]
        y_ref[b * nt:(b + 1) * nt, GROUP_WIDTH:2 * GROUP_WIDTH] = ob_ref[b] * gb_ref[b]
    y0 = jnp.dot(y_ref[...], w_attn_ref[...], preferred_element_type=F32)

    for b in range(nb):
        x1 = x_ref[b] + mod0_ref[b, :, 2 * D_MODEL:3 * D_MODEL] * y0[b * nt:(b + 1) * nt, :]
        x1_ref[b] = x1
        shift = mod_ref[b, :, 0:D_MODEL]
        scale = mod_ref[b, :, D_MODEL:2 * D_MODEL]
        h = _rms(x1, g_ref[...]) * (1.0 + scale) + shift
        h_ref[b * nt:(b + 1) * nt, :] = h.astype(BF16)

    xb = jnp.dot(h_ref[...], w_in_ref[:, 0:D_MODEL], preferred_element_type=F32)
    for b in range(nb):
        xb_ref[b, hist:hist + nt, :] = xb[b * nt:(b + 1) * nt, :]
    gate_branch = jnp.dot(h_ref[...], w_in_ref[:, D_MODEL:2 * D_MODEL], preferred_element_type=F32)
    sg_ref[...] = jax.nn.silu(gate_branch)

    for b in range(nb):
        xc = conv_b_ref[...] + conv_w_ref[CONV_WIDTH - 1:CONV_WIDTH, :] * xb_ref[b, hist:hist + nt, :]
        for tap in range(CONV_WIDTH - 1):
            off = hist - (CONV_WIDTH - 1) + tap
            xc = xc + conv_w_ref[tap:tap + 1, :] * xb_ref[b, off:off + nt, :]
        xc_ref[b * nt:(b + 1) * nt, :] = xc
        xb_ref[b, 0:hist, :] = xb_ref[b, nt:nt + hist, :]

    for n in range(LRU_BLOCKS):
        cols = slice(n * width, (n + 1) * width)
        xc_n = xc_ref[:, cols]
        gates = jnp.dot(xc_n.astype(BF16), w_gate_ref[n], preferred_element_type=F32)
        r = jax.nn.sigmoid(gates[:, 0:width] + b_a_ref[:, cols])
        inp = jax.nn.sigmoid(gates[:, width:2 * width] + b_x_ref[:, cols])
        log_a = LRU_C * r * jax.nn.log_sigmoid(lam_ref[:, cols])
        a = jnp.exp(log_a)
        gain_sq = -jnp.tanh(log_a) * (a * a + 1.0)
        mult = jnp.where(gain_sq > 0.0, gain_sq * lax.rsqrt(gain_sq), 0.0)
        b_term = mult * (inp * xc_n)
        for b in range(nb):
            a_ref[n, b * pitch:b * pitch + nt, :] = a[b * nt:(b + 1) * nt, :]
            b_ref[n, b * pitch:b * pitch + nt, :] = b_term[b * nt:(b + 1) * nt, :]

    def scan_step(t, hs):
        new = []
        for n in range(LRU_BLOCKS):
            rows = pl.ds(t, nb, stride=pitch)
            h_n = a_ref[n, rows, :] * hs[n] + b_ref[n, rows, :]
            b_ref[n, rows, :] = h_n
            new.append(h_n)
        return tuple(new)

    hs = lax.fori_loop(0, nt, scan_step, tuple(state_ref[n] for n in range(LRU_BLOCKS)), unroll=4)
    for n in range(LRU_BLOCKS):
        state_ref[n] = hs[n]

    for b in range(nb):
        hs_b = jnp.concatenate(
            [b_ref[n, b * pitch:b * pitch + nt, :] for n in range(LRU_BLOCKS)], axis=1)
        y_ref[b * nt:(b + 1) * nt, :] = (hs_b * sg_ref[b * nt:(b + 1) * nt, :]).astype(BF16)

    y = jnp.dot(y_ref[...], w_out_ref[...], preferred_element_type=F32)
    for b in range(nb):
        gate = mod_ref[b, :, 2 * D_MODEL:3 * D_MODEL]
        x_new = x1_ref[b] + gate * y[b * nt:(b + 1) * nt, :]
        o_ref[b] = _rms(x_new, final_g_ref[...])


def _lru_layer(x, mod, o_a, o_b, g_a, g_b, w_attn_out, norm_g, w_in, conv_w, conv_b, w_a, b_a, w_x, b_x,
               lam, w_out, final_g):
    batch, seq, _ = x.shape
    rows = LRU_BATCH * LRU_STEPS
    x_spec = pl.BlockSpec((LRU_BATCH, LRU_STEPS, D_MODEL), lambda g, i: (g, i, 0))
    half_spec = pl.BlockSpec((LRU_BATCH, LRU_STEPS, GROUP_WIDTH), lambda g, i: (g, i, 0))
    vec_spec = pl.BlockSpec((1, D_MODEL), lambda g, i: (0, 0))
    mod4 = mod.reshape(DEPTH, batch, 1, 3 * D_MODEL)
    w_gate = jnp.concatenate([w_a, w_x], axis=-1).astype(BF16)
    return pl.pallas_call(
        _lru_kernel,
        grid=(batch // LRU_BATCH, seq // LRU_STEPS),
        in_specs=[
            x_spec,
            pl.BlockSpec((None, LRU_BATCH, 1, 3 * D_MODEL), lambda g, i: (0, g, 0, 0)),
            half_spec, half_spec, half_spec, half_spec,
            _resident((D_MODEL, D_MODEL), lambda g, i: (0, 0)),
            pl.BlockSpec((None, LRU_BATCH, 1, 3 * D_MODEL), lambda g, i: (1, g, 0, 0)),
            pl.BlockSpec((None, 1, D_MODEL), lambda g, i: (1, 0, 0)),
            _resident((D_MODEL, 2 * D_MODEL), lambda g, i: (0, 0)),
            pl.BlockSpec((CONV_WIDTH, D_MODEL), lambda g, i: (0, 0)),
            vec_spec,
            _resident((LRU_BLOCKS, LRU_BLOCK_WIDTH, 2 * LRU_BLOCK_WIDTH), lambda g, i: (0, 0, 0)),
            vec_spec, vec_spec, vec_spec,
            _resident((D_MODEL, D_MODEL), lambda g, i: (0, 0)),
            vec_spec,
        ],
        out_specs=x_spec,
        out_shape=jax.ShapeDtypeStruct(x.shape, F32),
        scratch_shapes=[
            pltpu.VMEM((LRU_BATCH, LRU_STEPS, D_MODEL), F32),
            pltpu.VMEM((rows, D_MODEL), BF16),
            pltpu.VMEM((LRU_BATCH, CONV_HISTORY + LRU_STEPS, D_MODEL), F32),
            pltpu.VMEM((rows, D_MODEL), F32),
            pltpu.VMEM((rows, D_MODEL), F32),
            pltpu.VMEM((LRU_BLOCKS, LRU_BATCH * LRU_PITCH, LRU_BLOCK_WIDTH), F32),
            pltpu.VMEM((LRU_BLOCKS, LRU_BATCH * LRU_PITCH, LRU_BLOCK_WIDTH), F32),
            pltpu.VMEM((rows, D_MODEL), BF16),
            pltpu.VMEM((LRU_BLOCKS, LRU_BATCH, LRU_BLOCK_WIDTH), F32),
        ],
        compiler_params=pltpu.CompilerParams(
            dimension_semantics=("parallel", "arbitrary"), vmem_limit_bytes=VMEM_LIMIT),
        name="rglru_layer_final_norm",
    )(x, mod4, o_a, o_b, g_a, g_b, w_attn_out.astype(BF16), mod4, norm_g.reshape(DEPTH, 1, D_MODEL),
      w_in.astype(BF16), conv_w, conv_b.reshape(1, D_MODEL), w_gate,
      b_a.reshape(1, D_MODEL), b_x.reshape(1, D_MODEL), lam.reshape(1, D_MODEL),
      w_out.astype(BF16), final_g.reshape(1, D_MODEL))


def kernel(x, c, positions, norm_g, w_mod, b_mod, attn_w_in, attn_w_out, lru_w_in, lru_conv_w,
           lru_conv_b, lru_w_a, lru_b_a, lru_w_x, lru_b_x, lru_lambda, lru_w_out, final_g):
    batch, seq, d_model = x.shape
    assert d_model == D_MODEL and DEPTH == 2
    assert batch % LRU_BATCH == 0 and seq % PROJ_ROWS == 0 and seq % LRU_STEPS == 0
    assert seq % MOBA_BLOCK == 0 and seq % (SB_SUB * SB_BLOCK) == 0
    mod = _modulation(c, w_mod, b_mod)
    q_a, k_a, v_a, q_b, k_b, vt_b, g_a, g_b = _attn_projection(x, mod, norm_g, positions, attn_w_in[0])
    o_a = _sb_attention(q_a, k_a, v_a)
    o_b = _moba_attention(q_b, k_b, vt_b)
    return _lru_layer(x, mod, o_a, o_b, g_a, g_b, attn_w_out[0], norm_g, lru_w_in[0], lru_conv_w[0],
                      lru_conv_b[0], lru_w_a[0], lru_b_a[0], lru_w_x[0], lru_b_x[0], lru_lambda[0],
                      lru_w_out[0], final_g)
```

```python
import functools

import jax
import jax.numpy as jnp
import numpy as np
from jax import lax
from jax.experimental import pallas as pl
from jax.experimental.pallas import tpu as pltpu

F32 = jnp.float32
BF16 = jnp.bfloat16

D_MODEL = 1024
DEPTH = 2
HEAD_DIM = 64
N_HEADS = 8
GROUP_WIDTH = N_HEADS * HEAD_DIM
LANES = 128
SUBLANES = 8
BF16_SUBLANES = 16
HEAD_PAIRS = GROUP_WIDTH // LANES
SB_BLOCK = 128
MOBA_BLOCK = 256
MOBA_TOPK = 3
ROPE_THETA = 500000.0
ROT_DIM = HEAD_DIM // 4
LRU_BLOCKS = 8
LRU_BLOCK_WIDTH = D_MODEL // LRU_BLOCKS
CONV_WIDTH = 4
LRU_C = 8.0
EPS = 1e-6

PROJ_ROWS = 1024
PROJ_GROUP = 256
LRU_BATCH = SUBLANES
LRU_STEPS = 64
LRU_PITCH = LRU_STEPS + SUBLANES
CONV_HISTORY = SUBLANES
VMEM_LIMIT = 56 * 1024 * 1024
F32_EXP_UNDERFLOW = -104.0
SB_SUB = 2
SB_DONE = 1e30
LOG2_E = 1.4426950408889634

_NT = (((1,), (1,)), ((), ()))


def _resident(shape, index_map):
    return pl.BlockSpec(shape, index_map, pipeline_mode=pl.Buffered(1))


def _sigmoid(x):
    return 0.5 * jnp.tanh(0.5 * x) + 0.5


def _rms(x, g):
    ms = jnp.mean(x * x, axis=-1, keepdims=True)
    return x * lax.rsqrt(ms + EPS) * g


def _mod_kernel(c_ref, w_ref, b_ref, o_ref):
    o_ref[...] = jnp.dot(c_ref[...], w_ref[...], preferred_element_type=F32,
                         precision=lax.Precision.HIGHEST) + b_ref[...]


def _modulation(c, w_mod, b_mod):
    batch = c.shape[0]
    return pl.pallas_call(
        _mod_kernel,
        grid=(DEPTH, 3),
        in_specs=[
            pl.BlockSpec((batch, D_MODEL), lambda l, j: (0, 0)),
            pl.BlockSpec((None, D_MODEL, D_MODEL), lambda l, j: (l, 0, j)),
            pl.BlockSpec((None, 1, D_MODEL), lambda l, j: (l, 0, j)),
        ],
        out_specs=pl.BlockSpec((None, batch, D_MODEL), lambda l, j: (l, 0, j)),
        out_shape=jax.ShapeDtypeStruct((DEPTH, batch, 3 * D_MODEL), F32),
        name="adaln_modulation",
    )(c, w_mod, b_mod.reshape(DEPTH, 1, 3 * D_MODEL))


def _proj_kernel(x_ref, mod_ref, g_ref, pos_ref, inv_ref, w_ref, wvt_ref,
                 qa_ref, ka_ref, va_ref, qb_ref, kb_ref, vbt_ref, ga_ref, gb_ref):
    shift = mod_ref[:, 0:D_MODEL]
    scale = mod_ref[:, D_MODEL:2 * D_MODEL]
    qk_scale = HEAD_DIM ** -0.5
    half = ROT_DIM // 2
    in_head = lax.broadcasted_iota(jnp.int32, (1, LANES), 1) % HEAD_DIM
    from_below = jnp.logical_and(in_head >= half, in_head < ROT_DIM)
    from_above = in_head < half
    first_head = lax.broadcasted_iota(jnp.int32, (1, GROUP_WIDTH), 1) % LANES < HEAD_DIM

    for rows in (slice(r, r + PROJ_GROUP) for r in range(0, PROJ_ROWS, PROJ_GROUP)):
        h = (_rms(x_ref[rows, :], g_ref[...]) * (1.0 + scale) + shift).astype(BF16)

        def proj(j):
            return jnp.dot(h, w_ref[:, j * GROUP_WIDTH:(j + 1) * GROUP_WIDTH],
                           preferred_element_type=F32)

        ang = pos_ref[rows, :].astype(F32) * inv_ref[...]
        cos, sin = jnp.cos(ang), jnp.sin(ang)
        sin_from_below = jnp.where(from_below, sin, 0.0)
        sin_from_above = jnp.where(from_above, -sin, 0.0)

        def rotary(t):
            out = []
            for p in range(HEAD_PAIRS):
                tp = t[:, p * LANES:(p + 1) * LANES]
                out.append(tp * cos + pltpu.roll(tp, half, 1) * sin_from_below
                           + pltpu.roll(tp, LANES - half, 1) * sin_from_above)
            return jnp.concatenate(out, axis=1)

        qb_ref[rows, :] = (rotary(proj(3)) * (qk_scale * LOG2_E)).astype(BF16)
        kb_ref[rows, :] = rotary(proj(4)).astype(BF16)
        qa_ref[rows, :] = (proj(0) * qk_scale).astype(BF16)
        for ref, t in ((ka_ref, proj(1).astype(BF16)), (va_ref, proj(2).astype(BF16))):
            head0 = jnp.where(first_head, t, jnp.zeros_like(t))
            head1 = jnp.where(first_head, jnp.zeros_like(t), t)
            for r in range(0, PROJ_GROUP, SB_BLOCK):
                dst = 2 * (rows.start + r)
                ref[dst:dst + SB_BLOCK, :] = head0[r:r + SB_BLOCK, :]
                ref[dst + SB_BLOCK:dst + 2 * SB_BLOCK, :] = head1[r:r + SB_BLOCK, :]
        vbt_ref[:, rows] = lax.dot_general(wvt_ref[...], h, _NT, preferred_element_type=F32).astype(BF16)
        for ref, t in ((ga_ref, proj(5)), (gb_ref, proj(6))):
            ref[rows, :] = (t * _sigmoid(t)).astype(BF16)


def _rope_inv_lanes():
    half = ROT_DIM // 2
    inv = ROPE_THETA ** (-np.arange(0, ROT_DIM, 2, dtype=np.float32) / ROT_DIM)
    per_head = np.zeros((HEAD_DIM,), np.float32)
    per_head[:half] = inv
    per_head[half:ROT_DIM] = inv
    return jnp.asarray(np.tile(per_head, LANES // HEAD_DIM).reshape(1, LANES))


def _attn_projection(x, mod, norm_g, positions, w_in):
    batch, seq, _ = x.shape
    gw = GROUP_WIDTH
    cols = lambda j: w_in[:, j * gw:(j + 1) * gw]
    w_all = jnp.concatenate(
        [cols(0), cols(1), cols(2), cols(3), cols(4), cols(6), cols(7)], axis=1).astype(BF16)
    w_vbt = cols(5).T.astype(BF16)
    n_cols = w_all.shape[1]
    out_spec = pl.BlockSpec((None, PROJ_ROWS, gw), lambda b, i: (b, i, 0))
    out_shape = jax.ShapeDtypeStruct((batch, seq, gw), BF16)
    out_specs = [out_spec] * 8
    out_shapes = [out_shape] * 8
    out_specs[5] = pl.BlockSpec((None, gw, PROJ_ROWS), lambda b, i: (b, 0, i))
    out_shapes[5] = jax.ShapeDtypeStruct((batch, gw, seq), BF16)
    for j in (1, 2):
        out_specs[j] = pl.BlockSpec((None, 2 * PROJ_ROWS, gw), lambda b, i: (b, i, 0))
        out_shapes[j] = jax.ShapeDtypeStruct((batch, 2 * seq, gw), BF16)
    return pl.pallas_call(
        _proj_kernel,
        grid=(batch, seq // PROJ_ROWS),
        in_specs=[
            pl.BlockSpec((None, PROJ_ROWS, D_MODEL), lambda b, i: (b, i, 0)),
            pl.BlockSpec((None, None, 1, 3 * D_MODEL), lambda b, i: (0, b, 0, 0)),
            pl.BlockSpec((None, 1, D_MODEL), lambda b, i: (0, 0, 0)),
            pl.BlockSpec((None, PROJ_ROWS, 1), lambda b, i: (b, i, 0)),
            pl.BlockSpec((1, LANES), lambda b, i: (0, 0)),
            _resident((D_MODEL, n_cols), lambda b, i: (0, 0)),
            _resident((gw, D_MODEL), lambda b, i: (0, 0)),
        ],
        out_specs=out_specs,
        out_shape=out_shapes,
        compiler_params=pltpu.CompilerParams(
            dimension_semantics=("parallel", "parallel"), vmem_limit_bytes=VMEM_LIMIT),
        name="attn_in_projection",
    )(x, mod.reshape(DEPTH, batch, 1, 3 * D_MODEL), norm_g.reshape(DEPTH, 1, D_MODEL),
      positions.reshape(batch, seq, 1), _rope_inv_lanes(), w_all, w_vbt)


def _head_lane_masks():
    lane = lax.broadcasted_iota(jnp.int32, (1, LANES), 1)
    return lane < HEAD_DIM, lane >= HEAD_DIM


def _sb_kernel(q_ref, k_ref, v_ref, o_ref, z_ref, logb_ref, later_ref, total_ref, run_ref, acc_ref):
    qt = pl.program_id(1)
    blk = SB_BLOCK
    row = lax.broadcasted_iota(jnp.int32, (blk, 2 * blk), 0)
    col = lax.broadcasted_iota(jnp.int32, (blk, 2 * blk), 1)
    below_diag = jnp.where(col < blk, col, col - blk) < row
    r = lax.broadcasted_iota(jnp.int32, (blk, blk), 0)
    c = lax.broadcasted_iota(jnp.int32, (blk, blk), 1)
    later_and_total = jnp.concatenate([(r > c).astype(BF16), jnp.ones((blk, blk), BF16)], axis=1)
    later_and_total = jnp.concatenate([later_and_total, later_and_total], axis=0)
    tiles = [(g, p) for g in range(SB_SUB) for p in range(HEAD_PAIRS)]
    q_tiles = [q_ref[g * blk:(g + 1) * blk, p * LANES:(p + 1) * LANES] for g, p in tiles]

    def key_rows(g, j):
        start = pl.multiple_of(jnp.maximum(qt * SB_SUB + g - j, 0) * (2 * blk), 2 * blk)
        return pl.ds(start, 2 * blk)

    def scores(j):
        for t, (g, p) in enumerate(tiles):
            k2 = k_ref[key_rows(g, j), p * LANES:(p + 1) * LANES]
            z_ref[t] = lax.dot_general(q_tiles[t], k2, _NT, preferred_element_type=F32)

    def sweep(j, diagonal):
        for t in range(len(tiles)):
            z = z_ref[t]
            soft = jnp.maximum(z, 0.0) + jnp.log(1.0 + jnp.exp2(jnp.abs(z) * -LOG2_E))
            logb_ref[t] = z - soft
            if diagonal:
                soft = jnp.where(below_diag, soft, 0.0)
            hi = soft.astype(BF16)
            lo = (soft - hi.astype(F32)).astype(BF16)
            for h in range(2):
                split = jnp.concatenate([hi[:, h * blk:(h + 1) * blk], lo[:, h * blk:(h + 1) * blk]], axis=1)
                sums = jnp.dot(split, later_and_total, preferred_element_type=F32)
                later_ref[t, :, h * blk:(h + 1) * blk] = sums[:, :blk]
                total_ref[t, :, h * blk:(h + 1) * blk] = sums[:, blk:]
        scores(j + 1)
        least = None
        for t, (g, p) in enumerate(tiles):
            run = run_ref[t]
            w = jnp.exp(logb_ref[t] - later_ref[t] - run)
            if diagonal:
                w = jnp.where(below_diag, w, 0.0)
            v2 = v_ref[key_rows(g, j), p * LANES:(p + 1) * LANES]
            acc_ref[t] += jnp.dot(w.astype(BF16), v2, preferred_element_type=F32)
            no_more_keys = qt * SB_SUB + g - j <= 0
            run = run + total_ref[t] + jnp.where(no_more_keys, SB_DONE, 0.0)
            run_ref[t] = run
            least = run if least is None else jnp.minimum(least, run)
        return jnp.min(least)

    acc_ref[...] = jnp.zeros_like(acc_ref)
    run_ref[...] = jnp.zeros_like(run_ref)
    scores(0)
    least = sweep(0, diagonal=True)

    def more(carry):
        _, least = carry
        return least < -F32_EXP_UNDERFLOW

    def next_sweep(carry):
        j, _ = carry
        return j + 1, sweep(j, diagonal=False)

    lax.while_loop(more, next_sweep, (jnp.int32(1), least))
    for t, (g, p) in enumerate(tiles):
        o_ref[g * blk:(g + 1) * blk, p * LANES:(p + 1) * LANES] = acc_ref[t].astype(o_ref.dtype)


def _sb_attention(q, k, v):
    batch, seq, width = q.shape
    rows = SB_SUB * SB_BLOCK
    n_tiles = SB_SUB * HEAD_PAIRS
    kv_spec = pl.BlockSpec((None, 2 * seq, width), lambda b, i: (b, 0, 0))
    q_spec = pl.BlockSpec((None, rows, width), lambda b, i: (b, i, 0))
    return pl.pallas_call(
        _sb_kernel,
        grid=(batch, seq // rows),
        in_specs=[q_spec, kv_spec, kv_spec],
        out_specs=q_spec,
        out_shape=jax.ShapeDtypeStruct(q.shape, BF16),
        scratch_shapes=[pltpu.VMEM((n_tiles, SB_BLOCK, 2 * SB_BLOCK), F32)] * 5
        + [pltpu.VMEM((n_tiles, SB_BLOCK, LANES), F32)],
        compiler_params=pltpu.CompilerParams(
            dimension_semantics=("parallel", "arbitrary"), vmem_limit_bytes=VMEM_LIMIT),
        name="stick_breaking_attention",
    )(q, k, v)


def _moba_kernel(q_ref, k_ref, vt_ref, o_ref, kmean_ref, sel_ref, s_ref, *, n_blocks):
    qi = pl.program_id(1)
    blk = MOBA_BLOCK
    ones_rows = BF16_SUBLANES

    @pl.when(qi == 0)
    def _():
        for n in range(n_blocks):
            kn = k_ref[n * blk:(n + 1) * blk, :].astype(F32)
            kmean_ref[n:n + 1, :] = jnp.mean(kn, axis=0, keepdims=True)

    lo_mask, hi_mask = _head_lane_masks()
    blk_row = lax.broadcasted_iota(jnp.int32, (n_blocks, blk), 0)
    blk_row_f = blk_row.astype(F32)
    past = blk_row < qi
    key_row = lax.broadcasted_iota(jnp.int32, (blk, blk), 0)
    query_col = lax.broadcasted_iota(jnp.int32, (blk, blk), 1)
    causal = key_row <= query_col

    q_heads = []
    for p in range(HEAD_PAIRS):
        q_pair = q_ref[:, p * LANES:(p + 1) * LANES]
        kmean = kmean_ref[:, p * LANES:(p + 1) * LANES].astype(BF16)
        for mask in (lo_mask, hi_mask):
            qh = jnp.where(mask, q_pair, jnp.zeros_like(q_pair))
            gate = lax.dot_general(kmean, qh, _NT, preferred_element_type=F32)
            gate = jnp.where(past, gate, -jnp.inf)
            selected = jnp.zeros((n_blocks, blk), F32)
            for _ in range(MOBA_TOPK):
                best = jnp.max(gate, axis=0, keepdims=True)
                first = jnp.min(jnp.where(gate == best, blk_row_f, float(n_blocks)), axis=0, keepdims=True)
                hit = blk_row_f == first
                selected = jnp.where(jnp.logical_and(hit, past), 1.0, selected)
                gate = jnp.where(hit, -jnp.inf, gate)
            sel_ref[len(q_heads)] = selected
            q_heads.append(qh)

    def head_block(h, kb, slot, state, vt_pair, diagonal):
        m, acc = state
        s = s_ref[slot, h]
        if diagonal:
            s = jnp.where(causal, s, -jnp.inf)
            m_new = jnp.maximum(m, jnp.max(s, axis=0, keepdims=True))
            shift = m_new
        else:
            picked = jnp.logical_and(sel_ref[h, pl.ds(jnp.maximum(kb, 0), 1), :] > 0.5, kb >= 0)
            m_new = jnp.where(picked, jnp.maximum(m, jnp.max(s, axis=0, keepdims=True)), m)
            shift = jnp.where(picked, m_new, jnp.inf)
        p = jnp.exp2(s - shift).astype(BF16)
        alpha = jnp.exp2(m - m_new)
        sub = (h % 2) * HEAD_DIM
        vt_aug = jnp.concatenate([vt_pair[sub:sub + HEAD_DIM, :], jnp.ones((ones_rows, blk), BF16)], axis=0)
        acc = alpha * acc + jnp.dot(vt_aug, p, preferred_element_type=F32)
        return m_new, acc

    def score_stage(kb, slot):
        start = pl.multiple_of(jnp.maximum(kb, 0) * blk, blk)
        for p in range(HEAD_PAIRS):
            k_pair = k_ref[pl.ds(start, blk), p * LANES:(p + 1) * LANES]
            for h in (2 * p, 2 * p + 1):
                s_ref[slot, h] = lax.dot_general(k_pair, q_heads[h], _NT, preferred_element_type=F32)

    def value_stage(kb, slot, states, diagonal):
        start = pl.multiple_of(jnp.maximum(kb, 0) * blk, blk)
        new = []
        for p in range(HEAD_PAIRS):
            vt_pair = vt_ref[p * LANES:(p + 1) * LANES, pl.ds(start, blk)]
            for h in (2 * p, 2 * p + 1):
                new.append(head_block(h, kb, slot, states[h], vt_pair, diagonal))
        return tuple(new)

    init = (jnp.full((1, blk), -jnp.inf, F32), jnp.zeros((HEAD_DIM + ones_rows, blk), F32))
    score_stage(qi, 0)
    score_stage(qi - 1, 1)
    states = value_stage(qi, 0, (init,) * N_HEADS, diagonal=True)

    def two_blocks(i, states):
        kb = qi - 1 - 2 * i
        score_stage(kb - 1, 0)
        states = value_stage(kb, 1, states, diagonal=False)
        score_stage(kb - 2, 1)
        return value_stage(kb - 1, 0, states, diagonal=False)

    states = lax.fori_loop(0, (qi + 1) // 2, two_blocks, states)
    for p in range(HEAD_PAIRS):
        outs = []
        for h in (2 * p, 2 * p + 1):
            acc = states[h][1]
            outs.append(acc[0:HEAD_DIM, :] / acc[HEAD_DIM:HEAD_DIM + 1, :])
        o_ref[:, p * LANES:(p + 1) * LANES] = jnp.concatenate(outs, axis=0).T.astype(o_ref.dtype)


def _moba_attention(q, k, vt):
    batch, seq, width = q.shape
    n_blocks = seq // MOBA_BLOCK
    q_spec = pl.BlockSpec((None, MOBA_BLOCK, width), lambda b, i: (b, i, 0))
    return pl.pallas_call(
        functools.partial(_moba_kernel, n_blocks=n_blocks),
        grid=(batch, n_blocks),
        in_specs=[q_spec,
                  pl.BlockSpec((None, seq, width), lambda b, i: (b, 0, 0)),
                  pl.BlockSpec((None, width, seq), lambda b, i: (b, 0, 0))],
        out_specs=q_spec,
        out_shape=jax.ShapeDtypeStruct(q.shape, BF16),
        scratch_shapes=[pltpu.VMEM((n_blocks, width), F32),
                        pltpu.VMEM((N_HEADS, n_blocks, MOBA_BLOCK), F32),
                        pltpu.VMEM((2, N_HEADS, MOBA_BLOCK, MOBA_BLOCK), F32)],
        compiler_params=pltpu.CompilerParams(
            dimension_semantics=("parallel", "arbitrary"), vmem_limit_bytes=VMEM_LIMIT),
        name="moba_attention",
    )(q, k, vt)


def _lru_kernel(x_ref, mod0_ref, oa_ref, ob_ref, ga_ref, gb_ref, w_attn_ref,
                mod_ref, g_ref, w_in_ref, conv_w_ref, conv_b_ref, w_gate_ref,
                b_a_ref, b_x_ref, lam_ref, w_out_ref, final_g_ref, o_ref,
                x1_ref, h_ref, xb_ref, xc_ref, sg_ref, a_ref, b_ref, y_ref, state_ref):
    step_i = pl.program_id(1)
    nb, nt, pitch, hist = LRU_BATCH, LRU_STEPS, LRU_PITCH, CONV_HISTORY
    width = LRU_BLOCK_WIDTH

    @pl.when(step_i == 0)
    def _():
        state_ref[...] = jnp.zeros_like(state_ref)
        xb_ref[:, 0:hist, :] = jnp.zeros((nb, hist, D_MODEL), F32)

    for b in range(nb):
        y_ref[b * nt:(b + 1) * nt, 0:GROUP_WIDTH] = oa_ref[b] * ga_ref[b]
        y_ref[b * nt:(b + 1) * nt, GROUP_WIDTH:2 * GROUP_WIDTH] = ob_ref[b] * gb_ref[b]
    y0 = jnp.dot(y_ref[...], w_attn_ref[...], preferred_element_type=F32)

    for b in range(nb):
        x1 = x_ref[b] + mod0_ref[b, :, 2 * D_MODEL:3 * D_MODEL] * y0[b * nt:(b + 1) * nt, :]
        x1_ref[b] = x1
        shift = mod_ref[b, :, 0:D_MODEL]
        scale = mod_ref[b, :, D_MODEL:2 * D_MODEL]
        h = _rms(x1, g_ref[...]) * (1.0 + scale) + shift
        h_ref[b * nt:(b + 1) * nt, :] = h.astype(BF16)

    xb = jnp.dot(h_ref[...], w_in_ref[:, 0:D_MODEL], preferred_element_type=F32)
    for b in range(nb):
        xb_ref[b, hist:hist + nt, :] = xb[b * nt:(b + 1) * nt, :]
    gate_branch = jnp.dot(h_ref[...], w_in_ref[:, D_MODEL:2 * D_MODEL], preferred_element_type=F32)
    sg_ref[...] = gate_branch * _sigmoid(gate_branch)

    for b in range(nb):
        xc = conv_b_ref[...] + conv_w_ref[CONV_WIDTH - 1:CONV_WIDTH, :] * xb_ref[b, hist:hist + nt, :]
        for tap in range(CONV_WIDTH - 1):
            off = hist - (CONV_WIDTH - 1) + tap
            xc = xc + conv_w_ref[tap:tap + 1, :] * xb_ref[b, off:off + nt, :]
        xc_ref[b * nt:(b + 1) * nt, :] = xc
        xb_ref[b, 0:hist, :] = xb_ref[b, nt:nt + hist, :]

    for n in range(LRU_BLOCKS):
        cols = slice(n * width, (n + 1) * width)
        xc_n = xc_ref[:, cols]
        gates = jnp.dot(xc_n.astype(BF16), w_gate_ref[n], preferred_element_type=F32)
        r = _sigmoid(gates[:, 0:width] + b_a_ref[:, cols])
        inp = _sigmoid(gates[:, width:2 * width] + b_x_ref[:, cols])
        log_a = LRU_C * r * jax.nn.log_sigmoid(lam_ref[:, cols])
        a = jnp.exp(log_a)
        gain_sq = -jnp.tanh(log_a) * (a * a + 1.0)
        mult = jnp.where(gain_sq > 0.0, gain_sq * lax.rsqrt(gain_sq), 0.0)
        b_term = mult * (inp * xc_n)
        for b in range(nb):
            a_ref[n, b * pitch:b * pitch + nt, :] = a[b * nt:(b + 1) * nt, :]
            b_ref[n, b * pitch:b * pitch + nt, :] = b_term[b * nt:(b + 1) * nt, :]

    def scan_step(t, hs):
        new = []
        for n in range(LRU_BLOCKS):
            rows = pl.ds(t, nb, stride=pitch)
            h_n = a_ref[n, rows, :] * hs[n] + b_ref[n, rows, :]
            b_ref[n, rows, :] = h_n
            new.append(h_n)
        return tuple(new)

    hs = lax.fori_loop(0, nt, scan_step, tuple(state_ref[n] for n in range(LRU_BLOCKS)), unroll=4)
    for n in range(LRU_BLOCKS):
        state_ref[n] = hs[n]

    for b in range(nb):
        hs_b = jnp.concatenate(
            [b_ref[n, b * pitch:b * pitch + nt, :] for n in range(LRU_BLOCKS)], axis=1)
        y_ref[b * nt:(b + 1) * nt, :] = (hs_b * sg_ref[b * nt:(b + 1) * nt, :]).astype(BF16)

    y = jnp.dot(y_ref[...], w_out_ref[...], preferred_element_type=F32)
    for b in range(nb):
        gate = mod_ref[b, :, 2 * D_MODEL:3 * D_MODEL]
        x_new = x1_ref[b] + gate * y[b * nt:(b + 1) * nt, :]
        o_ref[b] = _rms(x_new, final_g_ref[...])


def _lru_layer(x, mod, o_a, o_b, g_a, g_b, w_attn_out, norm_g, w_in, conv_w, conv_b, w_a, b_a, w_x, b_x,
               lam, w_out, final_g):
    batch, seq, _ = x.shape
    rows = LRU_BATCH * LRU_STEPS
    x_spec = pl.BlockSpec((LRU_BATCH, LRU_STEPS, D_MODEL), lambda g, i: (g, i, 0))
    half_spec = pl.BlockSpec((LRU_BATCH, LRU_STEPS, GROUP_WIDTH), lambda g, i: (g, i, 0))
    vec_spec = pl.BlockSpec((1, D_MODEL), lambda g, i: (0, 0))
    mod4 = mod.reshape(DEPTH, batch, 1, 3 * D_MODEL)
    w_gate = jnp.concatenate([w_a, w_x], axis=-1).astype(BF16)
    return pl.pallas_call(
        _lru_kernel,
        grid=(batch // LRU_BATCH, seq // LRU_STEPS),
        in_specs=[
            x_spec,
            pl.BlockSpec((None, LRU_BATCH, 1, 3 * D_MODEL), lambda g, i: (0, g, 0, 0)),
            half_spec, half_spec, half_spec, half_spec,
            _resident((D_MODEL, D_MODEL), lambda g, i: (0, 0)),
            pl.BlockSpec((None, LRU_BATCH, 1, 3 * D_MODEL), lambda g, i: (1, g, 0, 0)),
            pl.BlockSpec((None, 1, D_MODEL), lambda g, i: (1, 0, 0)),
            _resident((D_MODEL, 2 * D_MODEL), lambda g, i: (0, 0)),
            pl.BlockSpec((CONV_WIDTH, D_MODEL), lambda g, i: (0, 0)),
            vec_spec,
            _resident((LRU_BLOCKS, LRU_BLOCK_WIDTH, 2 * LRU_BLOCK_WIDTH), lambda g, i: (0, 0, 0)),
            vec_spec, vec_spec, vec_spec,
            _resident((D_MODEL, D_MODEL), lambda g, i: (0, 0)),
            vec_spec,
        ],
        out_specs=x_spec,
        out_shape=jax.ShapeDtypeStruct(x.shape, F32),
        scratch_shapes=[
            pltpu.VMEM((LRU_BATCH, LRU_STEPS, D_MODEL), F32),
            pltpu.VMEM((rows, D_MODEL), BF16),
            pltpu.VMEM((LRU_BATCH, CONV_HISTORY + LRU_STEPS, D_MODEL), F32),
            pltpu.VMEM((rows, D_MODEL), F32),
            pltpu.VMEM((rows, D_MODEL), F32),
            pltpu.VMEM((LRU_BLOCKS, LRU_BATCH * LRU_PITCH, LRU_BLOCK_WIDTH), F32),
            pltpu.VMEM((LRU_BLOCKS, LRU_BATCH * LRU_PITCH, LRU_BLOCK_WIDTH), F32),
            pltpu.VMEM((rows, D_MODEL), BF16),
            pltpu.VMEM((LRU_BLOCKS, LRU_BATCH, LRU_BLOCK_WIDTH), F32),
        ],
        compiler_params=pltpu.CompilerParams(
            dimension_semantics=("parallel", "arbitrary"), vmem_limit_bytes=VMEM_LIMIT),
        name="rglru_layer_final_norm",
    )(x, mod4, o_a, o_b, g_a, g_b, w_attn_out.astype(BF16), mod4, norm_g.reshape(DEPTH, 1, D_MODEL),
      w_in.astype(BF16), conv_w, conv_b.reshape(1, D_MODEL), w_gate,
      b_a.reshape(1, D_MODEL), b_x.reshape(1, D_MODEL), lam.reshape(1, D_MODEL),
      w_out.astype(BF16), final_g.reshape(1, D_MODEL))


def kernel(x, c, positions, norm_g, w_mod, b_mod, attn_w_in, attn_w_out, lru_w_in, lru_conv_w,
           lru_conv_b, lru_w_a, lru_b_a, lru_w_x, lru_b_x, lru_lambda, lru_w_out, final_g):
    batch, seq, d_model = x.shape
    assert d_model == D_MODEL and DEPTH == 2
    assert batch % LRU_BATCH == 0 and seq % PROJ_ROWS == 0 and seq % LRU_STEPS == 0
    assert seq % MOBA_BLOCK == 0 and seq % (SB_SUB * SB_BLOCK) == 0
    mod = _modulation(c, w_mod, b_mod)
    q_a, k_a, v_a, q_b, k_b, vt_b, g_a, g_b = _attn_projection(x, mod, norm_g, positions, attn_w_in[0])
    o_a = _sb_attention(q_a, k_a, v_a)
    o_b = _moba_attention(q_b, k_b, vt_b)
    return _lru_layer(x, mod, o_a, o_b, g_a, g_b, attn_w_out[0], norm_g, lru_w_in[0], lru_conv_w[0],
                      lru_conv_b[0], lru_w_a[0], lru_b_a[0], lru_w_x[0], lru_b_x[0], lru_lambda[0],
                      lru_w_out[0], final_g)
```

```python
import functools

import jax
import jax.numpy as jnp
import numpy as np
from jax import lax
from jax.experimental import pallas as pl
from jax.experimental.pallas import tpu as pltpu

F32 = jnp.float32
BF16 = jnp.bfloat16

D_MODEL = 1024
DEPTH = 2
HEAD_DIM = 64
N_HEADS = 8
GROUP_WIDTH = N_HEADS * HEAD_DIM
LANES = 128
SUBLANES = 8
BF16_SUBLANES = 16
HEAD_PAIRS = GROUP_WIDTH // LANES
SB_BLOCK = 128
MOBA_BLOCK = 256
MOBA_TOPK = 3
ROPE_THETA = 500000.0
ROT_DIM = HEAD_DIM // 4
LRU_BLOCKS = 8
LRU_BLOCK_WIDTH = D_MODEL // LRU_BLOCKS
CONV_WIDTH = 4
LRU_C = 8.0
EPS = 1e-6

PROJ_ROWS = 1024
PROJ_GROUP = 256
LRU_BATCH = SUBLANES
LRU_STEPS = 64
LRU_PITCH = LRU_STEPS + SUBLANES
CONV_HISTORY = SUBLANES
VMEM_LIMIT = 56 * 1024 * 1024
F32_EXP_UNDERFLOW = -104.0
SB_SUB = 4
SB_DONE = 1e30
LOG2_E = 1.4426950408889634

_NT = (((1,), (1,)), ((), ()))


def _resident(shape, index_map):
    return pl.BlockSpec(shape, index_map, pipeline_mode=pl.Buffered(1))


def _sigmoid(x):
    return 0.5 * jnp.tanh(0.5 * x) + 0.5


def _rms(x, g):
    ms = jnp.mean(x * x, axis=-1, keepdims=True)
    return x * lax.rsqrt(ms + EPS) * g


def _mod_kernel(c_ref, w_ref, b_ref, o_ref):
    o_ref[...] = jnp.dot(c_ref[...], w_ref[...], preferred_element_type=F32,
                         precision=lax.Precision.HIGHEST) + b_ref[...]


def _modulation(c, w_mod, b_mod):
    batch = c.shape[0]
    return pl.pallas_call(
        _mod_kernel,
        grid=(DEPTH, 3),
        in_specs=[
            pl.BlockSpec((batch, D_MODEL), lambda l, j: (0, 0)),
            pl.BlockSpec((None, D_MODEL, D_MODEL), lambda l, j: (l, 0, j)),
            pl.BlockSpec((None, 1, D_MODEL), lambda l, j: (l, 0, j)),
        ],
        out_specs=pl.BlockSpec((None, batch, D_MODEL), lambda l, j: (l, 0, j)),
        out_shape=jax.ShapeDtypeStruct((DEPTH, batch, 3 * D_MODEL), F32),
        name="adaln_modulation",
    )(c, w_mod, b_mod.reshape(DEPTH, 1, 3 * D_MODEL))


def _proj_kernel(x_ref, mod_ref, g_ref, pos_ref, inv_ref, w_ref, wvt_ref,
                 qa_ref, ka_ref, va_ref, qb_ref, kb_ref, vbt_ref, ga_ref, gb_ref):
    shift = mod_ref[:, 0:D_MODEL]
    scale = mod_ref[:, D_MODEL:2 * D_MODEL]
    qk_scale = HEAD_DIM ** -0.5
    half = ROT_DIM // 2
    in_head = lax.broadcasted_iota(jnp.int32, (1, LANES), 1) % HEAD_DIM
    from_below = jnp.logical_and(in_head >= half, in_head < ROT_DIM)
    from_above = in_head < half
    first_head = lax.broadcasted_iota(jnp.int32, (1, GROUP_WIDTH), 1) % LANES < HEAD_DIM

    for rows in (slice(r, r + PROJ_GROUP) for r in range(0, PROJ_ROWS, PROJ_GROUP)):
        h = (_rms(x_ref[rows, :], g_ref[...]) * (1.0 + scale) + shift).astype(BF16)

        def proj(j):
            return jnp.dot(h, w_ref[:, j * GROUP_WIDTH:(j + 1) * GROUP_WIDTH],
                           preferred_element_type=F32)

        ang = pos_ref[rows, :].astype(F32) * inv_ref[...]
        cos, sin = jnp.cos(ang), jnp.sin(ang)
        sin_from_below = jnp.where(from_below, sin, 0.0)
        sin_from_above = jnp.where(from_above, -sin, 0.0)

        def rotary(t):
            out = []
            for p in range(HEAD_PAIRS):
                tp = t[:, p * LANES:(p + 1) * LANES]
                out.append(tp * cos + pltpu.roll(tp, half, 1) * sin_from_below
                           + pltpu.roll(tp, LANES - half, 1) * sin_from_above)
            return jnp.concatenate(out, axis=1)

        qb_ref[rows, :] = (rotary(proj(3)) * (qk_scale * LOG2_E)).astype(BF16)
        kb_ref[rows, :] = rotary(proj(4)).astype(BF16)
        qa_ref[rows, :] = (proj(0) * qk_scale).astype(BF16)
        for ref, t in ((ka_ref, proj(1).astype(BF16)), (va_ref, proj(2).astype(BF16))):
            head0 = jnp.where(first_head, t, jnp.zeros_like(t))
            head1 = jnp.where(first_head, jnp.zeros_like(t), t)
            for r in range(0, PROJ_GROUP, SB_BLOCK):
                dst = 2 * (rows.start + r)
                ref[dst:dst + SB_BLOCK, :] = head0[r:r + SB_BLOCK, :]
                ref[dst + SB_BLOCK:dst + 2 * SB_BLOCK, :] = head1[r:r + SB_BLOCK, :]
        vbt_ref[:, rows] = lax.dot_general(wvt_ref[...], h, _NT, preferred_element_type=F32).astype(BF16)
        for ref, t in ((ga_ref, proj(5)), (gb_ref, proj(6))):
            ref[rows, :] = (t * _sigmoid(t)).astype(BF16)


def _rope_inv_lanes():
    half = ROT_DIM // 2
    inv = ROPE_THETA ** (-np.arange(0, ROT_DIM, 2, dtype=np.float32) / ROT_DIM)
    per_head = np.zeros((HEAD_DIM,), np.float32)
    per_head[:half] = inv
    per_head[half:ROT_DIM] = inv
    return jnp.asarray(np.tile(per_head, LANES // HEAD_DIM).reshape(1, LANES))


def _attn_projection(x, mod, norm_g, positions, w_in):
    batch, seq, _ = x.shape
    gw = GROUP_WIDTH
    cols = lambda j: w_in[:, j * gw:(j + 1) * gw]
    w_all = jnp.concatenate(
        [cols(0), cols(1), cols(2), cols(3), cols(4), cols(6), cols(7)], axis=1).astype(BF16)
    w_vbt = cols(5).T.astype(BF16)
    n_cols = w_all.shape[1]
    out_spec = pl.BlockSpec((None, PROJ_ROWS, gw), lambda b, i: (b, i, 0))
    out_shape = jax.ShapeDtypeStruct((batch, seq, gw), BF16)
    out_specs = [out_spec] * 8
    out_shapes = [out_shape] * 8
    out_specs[5] = pl.BlockSpec((None, gw, PROJ_ROWS), lambda b, i: (b, 0, i))
    out_shapes[5] = jax.ShapeDtypeStruct((batch, gw, seq), BF16)
    for j in (1, 2):
        out_specs[j] = pl.BlockSpec((None, 2 * PROJ_ROWS, gw), lambda b, i: (b, i, 0))
        out_shapes[j] = jax.ShapeDtypeStruct((batch, 2 * seq, gw), BF16)
    return pl.pallas_call(
        _proj_kernel,
        grid=(batch, seq // PROJ_ROWS),
        in_specs=[
            pl.BlockSpec((None, PROJ_ROWS, D_MODEL), lambda b, i: (b, i, 0)),
            pl.BlockSpec((None, None, 1, 3 * D_MODEL), lambda b, i: (0, b, 0, 0)),
            pl.BlockSpec((None, 1, D_MODEL), lambda b, i: (0, 0, 0)),
            pl.BlockSpec((None, PROJ_ROWS, 1), lambda b, i: (b, i, 0)),
            pl.BlockSpec((1, LANES), lambda b, i: (0, 0)),
            _resident((D_MODEL, n_cols), lambda b, i: (0, 0)),
            _resident((gw, D_MODEL), lambda b, i: (0, 0)),
        ],
        out_specs=out_specs,
        out_shape=out_shapes,
        compiler_params=pltpu.CompilerParams(
            dimension_semantics=("parallel", "parallel"), vmem_limit_bytes=VMEM_LIMIT),
        name="attn_in_projection",
    )(x, mod.reshape(DEPTH, batch, 1, 3 * D_MODEL), norm_g.reshape(DEPTH, 1, D_MODEL),
      positions.reshape(batch, seq, 1), _rope_inv_lanes(), w_all, w_vbt)


def _head_lane_masks():
    lane = lax.broadcasted_iota(jnp.int32, (1, LANES), 1)
    return lane < HEAD_DIM, lane >= HEAD_DIM


def _sb_kernel(q_ref, k_ref, v_ref, o_ref, z_ref, logb_ref, later_ref, total_ref, run_ref, acc_ref):
    qt = pl.program_id(1)
    blk = SB_BLOCK
    row = lax.broadcasted_iota(jnp.int32, (blk, 2 * blk), 0)
    col = lax.broadcasted_iota(jnp.int32, (blk, 2 * blk), 1)
    below_diag = jnp.where(col < blk, col, col - blk) < row
    r = lax.broadcasted_iota(jnp.int32, (blk, blk), 0)
    c = lax.broadcasted_iota(jnp.int32, (blk, blk), 1)
    later_and_total = jnp.concatenate([(r > c).astype(BF16), jnp.ones((blk, blk), BF16)], axis=1)
    later_and_total = jnp.concatenate([later_and_total, later_and_total], axis=0)
    tiles = [(g, p) for g in range(SB_SUB) for p in range(HEAD_PAIRS)]
    q_tiles = [q_ref[g * blk:(g + 1) * blk, p * LANES:(p + 1) * LANES] for g, p in tiles]

    def key_rows(g, j):
        start = pl.multiple_of(jnp.maximum(qt * SB_SUB + g - j, 0) * (2 * blk), 2 * blk)
        return pl.ds(start, 2 * blk)

    def scores(j):
        for t, (g, p) in enumerate(tiles):
            k2 = k_ref[key_rows(g, j), p * LANES:(p + 1) * LANES]
            z_ref[t] = lax.dot_general(q_tiles[t], k2, _NT, preferred_element_type=F32)

    def sweep(j, diagonal):
        for t in range(len(tiles)):
            z = z_ref[t]
            soft = jnp.maximum(z, 0.0) + jnp.log(1.0 + jnp.exp2(jnp.abs(z) * -LOG2_E))
            logb_ref[t] = z - soft
            if diagonal:
                soft = jnp.where(below_diag, soft, 0.0)
            hi = soft.astype(BF16)
            lo = (soft - hi.astype(F32)).astype(BF16)
            for h in range(2):
                split = jnp.concatenate([hi[:, h * blk:(h + 1) * blk], lo[:, h * blk:(h + 1) * blk]], axis=1)
                sums = jnp.dot(split, later_and_total, preferred_element_type=F32)
                later_ref[t, :, h * blk:(h + 1) * blk] = sums[:, :blk]
                total_ref[t, :, h * blk:(h + 1) * blk] = sums[:, blk:]
        scores(j + 1)
        least = None
        for t, (g, p) in enumerate(tiles):
            run = run_ref[t]
            w = jnp.exp(logb_ref[t] - later_ref[t] - run)
            if diagonal:
                w = jnp.where(below_diag, w, 0.0)
            v2 = v_ref[key_rows(g, j), p * LANES:(p + 1) * LANES]
            acc_ref[t] += jnp.dot(w.astype(BF16), v2, preferred_element_type=F32)
            no_more_keys = qt * SB_SUB + g - j <= 0
            run = run + total_ref[t] + jnp.where(no_more_keys, SB_DONE, 0.0)
            run_ref[t] = run
            least = run if least is None else jnp.minimum(least, run)
        return jnp.min(least)

    acc_ref[...] = jnp.zeros_like(acc_ref)
    run_ref[...] = jnp.zeros_like(run_ref)
    scores(0)
    least = sweep(0, diagonal=True)

    def more(carry):
        _, least = carry
        return least < -F32_EXP_UNDERFLOW

    def next_sweep(carry):
        j, _ = carry
        return j + 1, sweep(j, diagonal=False)

    lax.while_loop(more, next_sweep, (jnp.int32(1), least))
    for t, (g, p) in enumerate(tiles):
        o_ref[g * blk:(g + 1) * blk, p * LANES:(p + 1) * LANES] = acc_ref[t].astype(o_ref.dtype)


def _sb_attention(q, k, v):
    batch, seq, width = q.shape
    rows = SB_SUB * SB_BLOCK
    n_tiles = SB_SUB * HEAD_PAIRS
    kv_spec = pl.BlockSpec((None, 2 * seq, width), lambda b, i: (b, 0, 0))
    q_spec = pl.BlockSpec((None, rows, width), lambda b, i: (b, i, 0))
    return pl.pallas_call(
        _sb_kernel,
        grid=(batch, seq // rows),
        in_specs=[q_spec, kv_spec, kv_spec],
        out_specs=q_spec,
        out_shape=jax.ShapeDtypeStruct(q.shape, BF16),
        scratch_shapes=[pltpu.VMEM((n_tiles, SB_BLOCK, 2 * SB_BLOCK), F32)] * 5
        + [pltpu.VMEM((n_tiles, SB_BLOCK, LANES), F32)],
        compiler_params=pltpu.CompilerParams(
            dimension_semantics=("parallel", "arbitrary"), vmem_limit_bytes=VMEM_LIMIT),
        name="stick_breaking_attention",
    )(q, k, v)


def _moba_kernel(q_ref, k_ref, vt_ref, o_ref, kmean_ref, sel_ref, s_ref, *, n_blocks):
    qi = pl.program_id(1)
    blk = MOBA_BLOCK
    ones_rows = BF16_SUBLANES

    @pl.when(qi == 0)
    def _():
        for n in range(n_blocks):
            kn = k_ref[n * blk:(n + 1) * blk, :].astype(F32)
            kmean_ref[n:n + 1, :] = jnp.mean(kn, axis=0, keepdims=True)

    lo_mask, hi_mask = _head_lane_masks()
    blk_row = lax.broadcasted_iota(jnp.int32, (n_blocks, blk), 0)
    blk_row_f = blk_row.astype(F32)
    past = blk_row < qi
    key_row = lax.broadcasted_iota(jnp.int32, (blk, blk), 0)
    query_col = lax.broadcasted_iota(jnp.int32, (blk, blk), 1)
    causal = key_row <= query_col

    q_heads = []
    for p in range(HEAD_PAIRS):
        q_pair = q_ref[:, p * LANES:(p + 1) * LANES]
        kmean = kmean_ref[:, p * LANES:(p + 1) * LANES].astype(BF16)
        for mask in (lo_mask, hi_mask):
            qh = jnp.where(mask, q_pair, jnp.zeros_like(q_pair))
            gate = lax.dot_general(kmean, qh, _NT, preferred_element_type=F32)
            gate = jnp.where(past, gate, -jnp.inf)
            selected = jnp.zeros((n_blocks, blk), F32)
            for _ in range(MOBA_TOPK):
                best = jnp.max(gate, axis=0, keepdims=True)
                first = jnp.min(jnp.where(gate == best, blk_row_f, float(n_blocks)), axis=0, keepdims=True)
                hit = blk_row_f == first
                selected = jnp.where(jnp.logical_and(hit, past), 1.0, selected)
                gate = jnp.where(hit, -jnp.inf, gate)
            sel_ref[len(q_heads)] = selected
            q_heads.append(qh)

    def head_block(h, kb, slot, state, vt_pair, diagonal):
        m, acc = state
        s = s_ref[slot, h]
        if diagonal:
            s = jnp.where(causal, s, -jnp.inf)
            m_new = jnp.maximum(m, jnp.max(s, axis=0, keepdims=True))
            shift = m_new
        else:
            picked = jnp.logical_and(sel_ref[h, pl.ds(jnp.maximum(kb, 0), 1), :] > 0.5, kb >= 0)
            m_new = jnp.where(picked, jnp.maximum(m, jnp.max(s, axis=0, keepdims=True)), m)
            shift = jnp.where(picked, m_new, jnp.inf)
        p = jnp.exp2(s - shift).astype(BF16)
        alpha = jnp.exp2(m - m_new)
        sub = (h % 2) * HEAD_DIM
        vt_aug = jnp.concatenate([vt_pair[sub:sub + HEAD_DIM, :], jnp.ones((ones_rows, blk), BF16)], axis=0)
        acc = alpha * acc + jnp.dot(vt_aug, p, preferred_element_type=F32)
        return m_new, acc

    def score_stage(kb, slot):
        start = pl.multiple_of(jnp.maximum(kb, 0) * blk, blk)
        for p in range(HEAD_PAIRS):
            k_pair = k_ref[pl.ds(start, blk), p * LANES:(p + 1) * LANES]
            for h in (2 * p, 2 * p + 1):
                s_ref[slot, h] = lax.dot_general(k_pair, q_heads[h], _NT, preferred_element_type=F32)

    def value_stage(kb, slot, states, diagonal):
        start = pl.multiple_of(jnp.maximum(kb, 0) * blk, blk)
        new = []
        for p in range(HEAD_PAIRS):
            vt_pair = vt_ref[p * LANES:(p + 1) * LANES, pl.ds(start, blk)]
            for h in (2 * p, 2 * p + 1):
                new.append(head_block(h, kb, slot, states[h], vt_pair, diagonal))
        return tuple(new)

    init = (jnp.full((1, blk), -jnp.inf, F32), jnp.zeros((HEAD_DIM + ones_rows, blk), F32))
    score_stage(qi, 0)
    score_stage(qi - 1, 1)
    states = value_stage(qi, 0, (init,) * N_HEADS, diagonal=True)

    def two_blocks(i, states):
        kb = qi - 1 - 2 * i
        score_stage(kb - 1, 0)
        states = value_stage(kb, 1, states, diagonal=False)
        score_stage(kb - 2, 1)
        return value_stage(kb - 1, 0, states, diagonal=False)

    states = lax.fori_loop(0, (qi + 1) // 2, two_blocks, states)
    for p in range(HEAD_PAIRS):
        outs = []
        for h in (2 * p, 2 * p + 1):
            acc = states[h][1]
            outs.append(acc[0:HEAD_DIM, :] / acc[HEAD_DIM:HEAD_DIM + 1, :])
        o_ref[:, p * LANES:(p + 1) * LANES] = jnp.concatenate(outs, axis=0).T.astype(o_ref.dtype)


def _moba_attention(q, k, vt):
    batch, seq, width = q.shape
    n_blocks = seq // MOBA_BLOCK
    q_spec = pl.BlockSpec((None, MOBA_BLOCK, width), lambda b, i: (b, i, 0))
    return pl.pallas_call(
        functools.partial(_moba_kernel, n_blocks=n_blocks),
        grid=(batch, n_blocks),
        in_specs=[q_spec,
                  pl.BlockSpec((None, seq, width), lambda b, i: (b, 0, 0)),
                  pl.BlockSpec((None, width, seq), lambda b, i: (b, 0, 0))],
        out_specs=q_spec,
        out_shape=jax.ShapeDtypeStruct(q.shape, BF16),
        scratch_shapes=[pltpu.VMEM((n_blocks, width), F32),
                        pltpu.VMEM((N_HEADS, n_blocks, MOBA_BLOCK), F32),
                        pltpu.VMEM((2, N_HEADS, MOBA_BLOCK, MOBA_BLOCK), F32)],
        compiler_params=pltpu.CompilerParams(
            dimension_semantics=("parallel", "arbitrary"), vmem_limit_bytes=VMEM_LIMIT),
        name="moba_attention",
    )(q, k, vt)


def _lru_kernel(x_ref, mod0_ref, oa_ref, ob_ref, ga_ref, gb_ref, w_attn_ref,
                mod_ref, g_ref, w_in_ref, conv_w_ref, conv_b_ref, w_gate_ref,
                b_a_ref, b_x_ref, lam_ref, w_out_ref, final_g_ref, o_ref,
                x1_ref, h_ref, xb_ref, xc_ref, sg_ref, a_ref, b_ref, y_ref, state_ref):
    step_i = pl.program_id(1)
    nb, nt, pitch, hist = LRU_BATCH, LRU_STEPS, LRU_PITCH, CONV_HISTORY
    width = LRU_BLOCK_WIDTH

    @pl.when(step_i == 0)
    def _():
        state_ref[...] = jnp.zeros_like(state_ref)
        xb_ref[:, 0:hist, :] = jnp.zeros((nb, hist, D_MODEL), F32)

    for b in range(nb):
        y_ref[b * nt:(b + 1) * nt, 0:GROUP_WIDTH] = oa_ref[b] * ga_ref[b]
        y_ref[b * nt:(b + 1) * nt, GROUP_WIDTH:2 * GROUP_WIDTH] = ob_ref[b] * gb_ref[b]
    y0 = jnp.dot(y_ref[...], w_attn_ref[...], preferred_element_type=F32)

    for b in range(nb):
        x1 = x_ref[b] + mod0_ref[b, :, 2 * D_MODEL:3 * D_MODEL] * y0[b * nt:(b + 1) * nt, :]
        x1_ref[b] = x1
        shift = mod_ref[b, :, 0:D_MODEL]
        scale = mod_ref[b, :, D_MODEL:2 * D_MODEL]
        h = _rms(x1, g_ref[...]) * (1.0 + scale) + shift
        h_ref[b * nt:(b + 1) * nt, :] = h.astype(BF16)

    xb = jnp.dot(h_ref[...], w_in_ref[:, 0:D_MODEL], preferred_element_type=F32)
    for b in range(nb):
        xb_ref[b, hist:hist + nt, :] = xb[b * nt:(b + 1) * nt, :]
    gate_branch = jnp.dot(h_ref[...], w_in_ref[:, D_MODEL:2 * D_MODEL], preferred_element_type=F32)
    sg_ref[...] = gate_branch * _sigmoid(gate_branch)

    for b in range(nb):
        xc = conv_b_ref[...] + conv_w_ref[CONV_WIDTH - 1:CONV_WIDTH, :] * xb_ref[b, hist:hist + nt, :]
        for tap in range(CONV_WIDTH - 1):
            off = hist - (CONV_WIDTH - 1) + tap
            xc = xc + conv_w_ref[tap:tap + 1, :] * xb_ref[b, off:off + nt, :]
        xc_ref[b * nt:(b + 1) * nt, :] = xc
        xb_ref[b, 0:hist, :] = xb_ref[b, nt:nt + hist, :]

    for n in range(LRU_BLOCKS):
        cols = slice(n * width, (n + 1) * width)
        xc_n = xc_ref[:, cols]
        gates = jnp.dot(xc_n.astype(BF16), w_gate_ref[n], preferred_element_type=F32)
        r = _sigmoid(gates[:, 0:width] + b_a_ref[:, cols])
        inp = _sigmoid(gates[:, width:2 * width] + b_x_ref[:, cols])
        log_a = LRU_C * r * jax.nn.log_sigmoid(lam_ref[:, cols])
        a = jnp.exp(log_a)
        gain_sq = -jnp.tanh(log_a) * (a * a + 1.0)
        mult = jnp.where(gain_sq > 0.0, gain_sq * lax.rsqrt(gain_sq), 0.0)
        b_term = mult * (inp * xc_n)
        for b in range(nb):
            a_ref[n, b * pitch:b * pitch + nt, :] = a[b * nt:(b + 1) * nt, :]
            b_ref[n, b * pitch:b * pitch + nt, :] = b_term[b * nt:(b + 1) * nt, :]

    def scan_step(t, hs):
        new = []
        for n in range(LRU_BLOCKS):
            rows = pl.ds(t, nb, stride=pitch)
            h_n = a_ref[n, rows, :] * hs[n] + b_ref[n, rows, :]
            b_ref[n, rows, :] = h_n
            new.append(h_n)
        return tuple(new)

    hs = lax.fori_loop(0, nt, scan_step, tuple(state_ref[n] for n in range(LRU_BLOCKS)), unroll=4)
    for n in range(LRU_BLOCKS):
        state_ref[n] = hs[n]

    for b in range(nb):
        hs_b = jnp.concatenate(
            [b_ref[n, b * pitch:b * pitch + nt, :] for n in range(LRU_BLOCKS)], axis=1)
        y_ref[b * nt:(b + 1) * nt, :] = (hs_b * sg_ref[b * nt:(b + 1) * nt, :]).astype(BF16)

    y = jnp.dot(y_ref[...], w_out_ref[...], preferred_element_type=F32)
    for b in range(nb):
        gate = mod_ref[b, :, 2 * D_MODEL:3 * D_MODEL]
        x_new = x1_ref[b] + gate * y[b * nt:(b + 1) * nt, :]
        o_ref[b] = _rms(x_new, final_g_ref[...])


def _lru_layer(x, mod, o_a, o_b, g_a, g_b, w_attn_out, norm_g, w_in, conv_w, conv_b, w_a, b_a, w_x, b_x,
               lam, w_out, final_g):
    batch, seq, _ = x.shape
    rows = LRU_BATCH * LRU_STEPS
    x_spec = pl.BlockSpec((LRU_BATCH, LRU_STEPS, D_MODEL), lambda g, i: (g, i, 0))
    half_spec = pl.BlockSpec((LRU_BATCH, LRU_STEPS, GROUP_WIDTH), lambda g, i: (g, i, 0))
    vec_spec = pl.BlockSpec((1, D_MODEL), lambda g, i: (0, 0))
    mod4 = mod.reshape(DEPTH, batch, 1, 3 * D_MODEL)
    w_gate = jnp.concatenate([w_a, w_x], axis=-1).astype(BF16)
    return pl.pallas_call(
        _lru_kernel,
        grid=(batch // LRU_BATCH, seq // LRU_STEPS),
        in_specs=[
            x_spec,
            pl.BlockSpec((None, LRU_BATCH, 1, 3 * D_MODEL), lambda g, i: (0, g, 0, 0)),
            half_spec, half_spec, half_spec, half_spec,
            _resident((D_MODEL, D_MODEL), lambda g, i: (0, 0)),
            pl.BlockSpec((None, LRU_BATCH, 1, 3 * D_MODEL), lambda g, i: (1, g, 0, 0)),
            pl.BlockSpec((None, 1, D_MODEL), lambda g, i: (1, 0, 0)),
            _resident((D_MODEL, 2 * D_MODEL), lambda g, i: (0, 0)),
            pl.BlockSpec((CONV_WIDTH, D_MODEL), lambda g, i: (0, 0)),
            vec_spec,
            _resident((LRU_BLOCKS, LRU_BLOCK_WIDTH, 2 * LRU_BLOCK_WIDTH), lambda g, i: (0, 0, 0)),
            vec_spec, vec_spec, vec_spec,
            _resident((D_MODEL, D_MODEL), lambda g, i: (0, 0)),
            vec_spec,
        ],
        out_specs=x_spec,
        out_shape=jax.ShapeDtypeStruct(x.shape, F32),
        scratch_shapes=[
            pltpu.VMEM((LRU_BATCH, LRU_STEPS, D_MODEL), F32),
            pltpu.VMEM((rows, D_MODEL), BF16),
            pltpu.VMEM((LRU_BATCH, CONV_HISTORY + LRU_STEPS, D_MODEL), F32),
            pltpu.VMEM((rows, D_MODEL), F32),
            pltpu.VMEM((rows, D_MODEL), F32),
            pltpu.VMEM((LRU_BLOCKS, LRU_BATCH * LRU_PITCH, LRU_BLOCK_WIDTH), F32),
            pltpu.VMEM((LRU_BLOCKS, LRU_BATCH * LRU_PITCH, LRU_BLOCK_WIDTH), F32),
            pltpu.VMEM((rows, D_MODEL), BF16),
            pltpu.VMEM((LRU_BLOCKS, LRU_BATCH, LRU_BLOCK_WIDTH), F32),
        ],
        compiler_params=pltpu.CompilerParams(
            dimension_semantics=("parallel", "arbitrary"), vmem_limit_bytes=VMEM_LIMIT),
        name="rglru_layer_final_norm",
    )(x, mod4, o_a, o_b, g_a, g_b, w_attn_out.astype(BF16), mod4, norm_g.reshape(DEPTH, 1, D_MODEL),
      w_in.astype(BF16), conv_w, conv_b.reshape(1, D_MODEL), w_gate,
      b_a.reshape(1, D_MODEL), b_x.reshape(1, D_MODEL), lam.reshape(1, D_MODEL),
      w_out.astype(BF16), final_g.reshape(1, D_MODEL))


def kernel(x, c, positions, norm_g, w_mod, b_mod, attn_w_in, attn_w_out, lru_w_in, lru_conv_w,
           lru_conv_b, lru_w_a, lru_b_a, lru_w_x, lru_b_x, lru_lambda, lru_w_out, final_g):
    batch, seq, d_model = x.shape
    assert d_model == D_MODEL and DEPTH == 2
    assert batch % LRU_BATCH == 0 and seq % PROJ_ROWS == 0 and seq % LRU_STEPS == 0
    assert seq % MOBA_BLOCK == 0 and seq % (SB_SUB * SB_BLOCK) == 0
    mod = _modulation(c, w_mod, b_mod)
    q_a, k_a, v_a, q_b, k_b, vt_b, g_a, g_b = _attn_projection(x, mod, norm_g, positions, attn_w_in[0])
    o_a = _sb_attention(q_a, k_a, v_a)
    o_b = _moba_attention(q_b, k_b, vt_b)
    return _lru_layer(x, mod, o_a, o_b, g_a, g_b, attn_w_out[0], norm_g, lru_w_in[0], lru_conv_w[0],
                      lru_conv_b[0], lru_w_a[0], lru_b_a[0], lru_w_x[0], lru_b_x[0], lru_lambda[0],
                      lru_w_out[0], final_g)
```
